```python
import jax, jax.numpy as jnp
from jax import lax
import numpy as np

D_MODEL = 2048
BATCH = 4
SEQ = 4096
DEPTH = 2

GLA_HEADS = 4
GLA_DK = D_MODEL // 2 // GLA_HEADS
GLA_DV = D_MODEL // GLA_HEADS
GLA_KEY_WIDTH = GLA_HEADS * GLA_DK
GLA_VAL_WIDTH = GLA_HEADS * GLA_DV
GLA_LOW_RANK = 16
GLA_TAU = 16.0
GLA_CHUNK = 64
GLA_NORM_EPS = 1e-5
LRU_WIDTH = D_MODEL
LRU_BLOCKS = 8
LRU_BLOCK = LRU_WIDTH // LRU_BLOCKS
LRU_C = 8.0
CONV_WIDTH = 4
D_FF = -(-8 * D_MODEL // (3 * 256)) * 256
N_BRANCHES = 2
DEEPNORM_ALPHA = (2 * DEPTH) ** 0.25
DEEPNORM_BETA = (8 * DEPTH) ** -0.25
LN_EPS = 1e-5

SPLIT_SIZES = (GLA_KEY_WIDTH, GLA_KEY_WIDTH, GLA_VAL_WIDTH, GLA_VAL_WIDTH, GLA_LOW_RANK,
               LRU_WIDTH, LRU_WIDTH, N_BRANCHES * D_MODEL)
SPLIT_POINTS = [sum(SPLIT_SIZES[:i + 1]) for i in range(len(SPLIT_SIZES) - 1)]
IN_COLS = sum(SPLIT_SIZES)

kernel_name = "hybrid_gla_rglru_deepnorm"


def layer_norm(x, g, b):
    xf = x.astype(jnp.float32)
    mu = jnp.mean(xf, axis=-1, keepdims=True)
    var = jnp.mean(jnp.square(xf - mu), axis=-1, keepdims=True)
    y = (xf - mu) * lax.rsqrt(var + LN_EPS)
    return (y * g + b).astype(x.dtype)


def rms_norm(x, g):
    xf = x.astype(jnp.float32)
    y = xf * lax.rsqrt(jnp.mean(jnp.square(xf), axis=-1, keepdims=True) + GLA_NORM_EPS)
    return (y * g).astype(x.dtype)


def gla_chunked(q, k, v, log_a):
    B, T, H, dk = q.shape
    dv = v.shape[-1]
    C = GLA_CHUNK
    N = T // C
    out_dtype = v.dtype

    def to_chunks(t):
        return t.reshape(B, N, C, H, t.shape[-1]).transpose(1, 0, 3, 2, 4).astype(jnp.float32)

    qc, kc, vc, gc = to_chunks(q), to_chunks(k), to_chunks(v), to_chunks(log_a)
    bcum = jnp.cumsum(gc, axis=3)
    b_last = bcum[:, :, :, -1:, :]
    q_dec = qc * jnp.exp(bcum)
    k_inv = kc * jnp.exp(-bcum)
    k_end = kc * jnp.exp(b_last - bcum)

    causal = jnp.tril(jnp.ones((C, C), dtype=bool))
    scores = jnp.einsum('nbhik,nbhjk->nbhij', q_dec, k_inv)
    scores = jnp.where(causal, scores, 0.0)
    o_intra = jnp.einsum('nbhij,nbhjv->nbhiv', scores, vc)

    def step(S, inp):
        q_d, k_e, v_c, b_l = inp
        o = jnp.einsum('bhik,bhkv->bhiv', q_d, S)
        S = S * jnp.exp(b_l)[:, :, 0, :, None] + jnp.einsum('bhjk,bhjv->bhkv', k_e, v_c)
        return S, o

    S0 = jnp.zeros((B, H, dk, dv), jnp.float32)
    _, o_inter = lax.scan(step, S0, (q_dec, k_end, vc, b_last))
    o = (o_intra + o_inter).transpose(1, 0, 3, 2, 4).reshape(B, T, H, dv)
    return o.astype(out_dtype)


def causal_depthwise_conv(x, w, b):
    K, W = w.shape
    y = lax.conv_general_dilated(x, w[:, None, :], window_strides=(1,), padding=[(K - 1, 0)],
                                 dimension_numbers=('NWC', 'WIO', 'NWC'), feature_group_count=W)
    return y + b


def rg_lru(x, w_a, b_a, w_x, b_x, lam):
    B, T, W = x.shape
    xb = x.reshape(B, T, LRU_BLOCKS, LRU_BLOCK)
    r = jax.nn.sigmoid(jnp.einsum('btni,nij->btnj', xb, w_a).reshape(B, T, W) + b_a)
    i = jax.nn.sigmoid(jnp.einsum('btni,nij->btnj', xb, w_x).reshape(B, T, W) + b_x)
    log_a = (-LRU_C * jax.nn.softplus(-lam.astype(jnp.float32))) * r.astype(jnp.float32)
    a = jnp.exp(log_a)
    u = jnp.sqrt(-jnp.expm1(2.0 * log_a)) * (i * x).astype(jnp.float32)

    def combine(left, right):
        a1, h1 = left
        a2, h2 = right
        return a1 * a2, a2 * h1 + h2

    _, h = lax.associative_scan(combine, (a, u), axis=1)
    return h.astype(x.dtype)


def hybrid_mixer(x, w_in, gla_w_alpha, gla_b_alpha, gla_norm_g, gla_w_o, conv_w, conv_b,
                 lru_w_a, lru_b_a, lru_w_x, lru_b_x, lru_lambda, rnn_w_o, w_out):
    B, T, _ = x.shape
    proj = jnp.einsum('btd,dc->btc', x, w_in)
    q, k, v, g_out, lr, rx, ry, gates = jnp.split(proj, SPLIT_POINTS, axis=-1)

    q = q.reshape(B, T, GLA_HEADS, GLA_DK) * (GLA_DK ** -0.5)
    k = k.reshape(B, T, GLA_HEADS, GLA_DK)
    v = v.reshape(B, T, GLA_HEADS, GLA_DV)
    alpha_pre = jnp.einsum('btr,rk->btk', lr, gla_w_alpha) + gla_b_alpha
    log_a = (jax.nn.log_sigmoid(alpha_pre.astype(jnp.float32)) / GLA_TAU).reshape(B, T, GLA_HEADS, GLA_DK)
    o = gla_chunked(q, k, v, log_a)
    o = rms_norm(o, gla_norm_g).reshape(B, T, GLA_VAL_WIDTH)
    y_a = jnp.einsum('btv,vd->btd', o * jax.nn.silu(g_out), gla_w_o)

    hx = causal_depthwise_conv(rx, conv_w, conv_b)
    hx = rg_lru(hx, lru_w_a, lru_b_a, lru_w_x, lru_b_x, lru_lambda)
    y_b = jnp.einsum('btw,wd->btd', hx * jax.nn.gelu(ry), rnn_w_o)

    g_a, g_b = jnp.split(jax.nn.sigmoid(gates), 2, axis=-1)
    merged = g_a * y_a + g_b * y_b
    return jnp.einsum('btd,de->bte', merged, w_out)


def swiglu_ffn(x, w_gate_up, w_down):
    h = jnp.einsum('btd,df->btf', x, w_gate_up)
    gate, up = jnp.split(h, 2, axis=-1)
    return jnp.einsum('btf,fd->btd', jax.nn.silu(gate) * up, w_down)


def setup_inputs(seed: int = 0) -> dict:
    key = jax.random.key(seed)
    ks = jax.random.split(key, 24)
    f32 = jnp.float32
    nrm = lambda k, shape, s: jax.random.normal(k, shape, f32) * s
    beta = DEEPNORM_BETA
    col_scale = jnp.concatenate([
        jnp.full((n,), s, f32) for n, s in zip(
            SPLIT_SIZES, (1.0, 1.0, beta, 1.0, 1.0, beta, 1.0, 1.0))])
    x = jax.random.normal(ks[0], (BATCH, SEQ, D_MODEL), f32)
    w_in = nrm(ks[1], (DEPTH, D_MODEL, IN_COLS), D_MODEL ** -0.5) * col_scale
    gla_w_alpha = nrm(ks[2], (DEPTH, GLA_LOW_RANK, GLA_KEY_WIDTH), GLA_LOW_RANK ** -0.5)
    gla_b_alpha = nrm(ks[3], (DEPTH, GLA_KEY_WIDTH), 0.1)
    gla_norm_g = 1.0 + nrm(ks[4], (DEPTH, GLA_DV), 0.02)
    gla_w_o = nrm(ks[5], (DEPTH, GLA_VAL_WIDTH, D_MODEL), GLA_VAL_WIDTH ** -0.5 * beta)
    conv_w = nrm(ks[6], (DEPTH, CONV_WIDTH, LRU_WIDTH), CONV_WIDTH ** -0.5)
    conv_b = nrm(ks[7], (DEPTH, LRU_WIDTH), 0.02)
    lru_w_a = nrm(ks[8], (DEPTH, LRU_BLOCKS, LRU_BLOCK, LRU_BLOCK), LRU_BLOCK ** -0.5)
    lru_b_a = nrm(ks[9], (DEPTH, LRU_WIDTH), 0.02)
    lru_w_x = nrm(ks[10], (DEPTH, LRU_BLOCKS, LRU_BLOCK, LRU_BLOCK), LRU_BLOCK ** -0.5)
    lru_b_x = nrm(ks[11], (DEPTH, LRU_WIDTH), 0.02)
    u = jax.random.uniform(ks[12], (DEPTH, LRU_WIDTH), f32, 0.9, 0.999)
    s = u ** (1.0 / LRU_C)
    lru_lambda = jnp.log(s) - jnp.log1p(-s)
    rnn_w_o = nrm(ks[13], (DEPTH, LRU_WIDTH, D_MODEL), LRU_WIDTH ** -0.5 * beta)
    w_out = nrm(ks[14], (DEPTH, D_MODEL, D_MODEL), D_MODEL ** -0.5 * beta)
    ln1_g = 1.0 + nrm(ks[15], (DEPTH, D_MODEL), 0.02)
    ln1_b = nrm(ks[16], (DEPTH, D_MODEL), 0.02)
    w_gate_up = nrm(ks[17], (DEPTH, D_MODEL, 2 * D_FF), D_MODEL ** -0.5)
    w_down = nrm(ks[18], (DEPTH, D_FF, D_MODEL), D_FF ** -0.5 * beta)
    ln2_g = 1.0 + nrm(ks[19], (DEPTH, D_MODEL), 0.02)
    ln2_b = nrm(ks[20], (DEPTH, D_MODEL), 0.02)
    return {"x": x, "w_in": w_in, "gla_w_alpha": gla_w_alpha, "gla_b_alpha": gla_b_alpha,
            "gla_norm_g": gla_norm_g, "gla_w_o": gla_w_o, "conv_w": conv_w, "conv_b": conv_b,
            "lru_w_a": lru_w_a, "lru_b_a": lru_b_a, "lru_w_x": lru_w_x, "lru_b_x": lru_b_x,
            "lru_lambda": lru_lambda, "rnn_w_o": rnn_w_o, "w_out": w_out,
            "ln1_g": ln1_g, "ln1_b": ln1_b, "w_gate_up": w_gate_up, "w_down": w_down,
            "ln2_g": ln2_g, "ln2_b": ln2_b}


def reference(x, w_in, gla_w_alpha, gla_b_alpha, gla_norm_g, gla_w_o, conv_w, conv_b,
              lru_w_a, lru_b_a, lru_w_x, lru_b_x, lru_lambda, rnn_w_o, w_out,
              ln1_g, ln1_b, w_gate_up, w_down, ln2_g, ln2_b):
    h = x
    for l in range(DEPTH):
        mix = hybrid_mixer(h, w_in[l], gla_w_alpha[l], gla_b_alpha[l], gla_norm_g[l], gla_w_o[l],
                           conv_w[l], conv_b[l], lru_w_a[l], lru_b_a[l], lru_w_x[l], lru_b_x[l],
                           lru_lambda[l], rnn_w_o[l], w_out[l])
        h = layer_norm(DEEPNORM_ALPHA * h + mix, ln1_g[l], ln1_b[l])
        h = layer_norm(DEEPNORM_ALPHA * h + swiglu_ffn(h, w_gate_up[l], w_down[l]), ln2_g[l], ln2_b[l])
    return h
```

```python
import functools

import jax
import jax.numpy as jnp
from jax import lax
from jax.experimental import pallas as pl
from jax.experimental.pallas import tpu as pltpu

F32 = jnp.float32
BF16 = jnp.bfloat16

GLA_HEADS = 4
GLA_LOW_RANK = 16
GLA_TAU = 16.0
GLA_CHUNK = 64
GLA_NORM_EPS = 1e-5
LRU_BLOCK = 256
LRU_C = 8.0
CONV_WIDTH = 4
LN_EPS = 1e-5

LANES = 128
SUBLANES = 8
VMEM_LIMIT_BYTES = 56 * 1024 * 1024


def _cparams(sem):
    return pltpu.CompilerParams(dimension_semantics=sem, vmem_limit_bytes=VMEM_LIMIT_BYTES)


def _dot(a, b):
    return jnp.dot(a, b, preferred_element_type=F32)


def _dot_nt(a, b):
    return lax.dot_general(a, b, (((1,), (1,)), ((), ())), preferred_element_type=F32)


def _dot_tn(a, b):
    return lax.dot_general(a, b, (((0,), (0,)), ((), ())), preferred_element_type=F32)


def _layer_norm(z, g, b):
    mu = jnp.mean(z, axis=-1, keepdims=True)
    zc = z - mu
    var = jnp.mean(zc * zc, axis=-1, keepdims=True)
    return zc * lax.rsqrt(var + LN_EPS) * g + b


def _in_proj_kernel(x_ref, w_ref, wlr_ref, o_ref, lr_ref, xb_ref):
    @pl.when(pl.program_id(1) == 0)
    def _():
        xb = x_ref[...].astype(BF16)
        xb_ref[...] = xb
        lr_ref[...] = _dot(xb, wlr_ref[...]).astype(BF16)

    o_ref[...] = _dot(xb_ref[...], w_ref[...]).astype(BF16)


def _in_proj(x, w, wlr, tm, tn):
    m, d = x.shape
    n = w.shape[1]
    return pl.pallas_call(
        _in_proj_kernel,
        grid=(m // tm, n // tn),
        in_specs=[
            pl.BlockSpec((tm, d), lambda i, j: (i, 0)),
            pl.BlockSpec((d, tn), lambda i, j: (0, j)),
            pl.BlockSpec((d, LANES), lambda i, j: (0, 0)),
        ],
        out_specs=[
            pl.BlockSpec((tm, tn), lambda i, j: (i, j)),
            pl.BlockSpec((tm, LANES), lambda i, j: (i, 0)),
        ],
        out_shape=[jax.ShapeDtypeStruct((m, n), BF16), jax.ShapeDtypeStruct((m, LANES), BF16)],
        scratch_shapes=[pltpu.VMEM((tm, d), BF16)],
        compiler_params=_cparams(("parallel", "arbitrary")),
        name="in_proj",
    )(x, w, wlr)


def _gla_kernel(q_ref, k_ref, v_ref, g_ref, lr_ref, wa_ref, ba_ref, ng_ref, o_ref, st_ref, *, scale):
    tc = q_ref.shape[0]
    c = GLA_CHUNK

    @pl.when(pl.program_id(2) == 0)
    def _():
        st_ref[...] = jnp.zeros_like(st_ref)

    alpha_pre = _dot(lr_ref[...], wa_ref[...]) + ba_ref[...]
    log_a = (jnp.minimum(alpha_pre, 0.0) - jnp.log1p(jnp.exp(-jnp.abs(alpha_pre)))) * (1.0 / GLA_TAU)

    row = lax.broadcasted_iota(jnp.int32, (c, c), 0)
    col = lax.broadcasted_iota(jnp.int32, (c, c), 1)
    causal = row >= col
    tri = jnp.where(causal, 1.0, 0.0).astype(BF16)

    for ci in range(tc // c):
        sl = slice(ci * c, (ci + 1) * c)
        la = log_a[sl]
        hi = la.astype(BF16)
        r1 = la - hi.astype(F32)
        mid = r1.astype(BF16)
        lo = (r1 - mid.astype(F32)).astype(BF16)
        bc = _dot(tri, hi) + _dot(tri, mid) + _dot(tri, lo)
        b_last = bc[c - 1:c, :]

        qc = q_ref[sl, :].astype(F32) * scale
        kc = k_ref[sl, :].astype(F32)
        vc = v_ref[sl, :]
        q_dec = (qc * jnp.exp(bc)).astype(BF16)
        k_inv = (kc * jnp.exp(-bc)).astype(BF16)
        k_end = (kc * jnp.exp(b_last - bc)).astype(BF16)

        scores = jnp.where(causal, _dot_nt(q_dec, k_inv), 0.0).astype(BF16)
        o_intra = _dot(scores, vc)
        st = st_ref[...]
        o_inter = _dot_nt(q_dec, st.astype(BF16))
        st_ref[...] = st * jnp.exp(b_last) + _dot_tn(vc, k_end)

        o = o_intra + o_inter
        ms = jnp.mean(o * o, axis=-1, keepdims=True)
        y = o * lax.rsqrt(ms + GLA_NORM_EPS) * ng_ref[...]
        g = g_ref[sl, :].astype(F32)
        o_ref[sl, :] = (y * (g * jax.nn.sigmoid(g))).astype(BF16)


def _gla(proj, lr, wa, ba, ng, batch, seq, tc, off):
    m = proj.shape[0]
    dk = wa.shape[1] // GLA_HEADS
    dv = ng.shape[1]
    nt = seq // tc
    rowmap = lambda b, h, t: b * nt + t
    return pl.pallas_call(
        functools.partial(_gla_kernel, scale=float(dk) ** -0.5),
        grid=(batch, GLA_HEADS, nt),
        in_specs=[
            pl.BlockSpec((tc, dk), lambda b, h, t: (rowmap(b, h, t), off["q"] // dk + h)),
            pl.BlockSpec((tc, dk), lambda b, h, t: (rowmap(b, h, t), off["k"] // dk + h)),
            pl.BlockSpec((tc, dv), lambda b, h, t: (rowmap(b, h, t), off["v"] // dv + h)),
            pl.BlockSpec((tc, dv), lambda b, h, t: (rowmap(b, h, t), off["g"] // dv + h)),
            pl.BlockSpec((tc, LANES), lambda b, h, t: (rowmap(b, h, t), 0)),
            pl.BlockSpec((LANES, dk), lambda b, h, t: (0, h)),
            pl.BlockSpec((1, dk), lambda b, h, t: (0, h)),
            pl.BlockSpec((1, dv), lambda b, h, t: (0, 0)),
        ],
        out_specs=pl.BlockSpec((tc, dv), lambda b, h, t: (rowmap(b, h, t), h)),
        out_shape=jax.ShapeDtypeStruct((m, GLA_HEADS * dv), BF16),
        scratch_shapes=[pltpu.VMEM((dv, dk), F32)],
        compiler_params=_cparams(("parallel", "parallel", "arbitrary")),
        name="gla",
    )(proj, proj, proj, proj, lr, wa, ba, ng)


def _lru_kernel(rx_ref, ry_ref, cw_ref, cb_ref, wa_ref, ba_ref, wx_ref, bx_ref, lam_ref, o_ref,
                xbuf_ref, a_ref, u_ref, hc_ref):
    tc, wc = rx_ref.shape
    gb = wc // LRU_BLOCK
    t = pl.program_id(2)

    @pl.when(t == 0)
    def _():
        xbuf_ref[0:SUBLANES, :] = jnp.zeros((SUBLANES, wc), F32)
        hc_ref[...] = jnp.zeros_like(hc_ref)

    @pl.when(t > 0)
    def _():
        xbuf_ref[0:SUBLANES, :] = xbuf_ref[tc:tc + SUBLANES, :]

    xbuf_ref[SUBLANES:SUBLANES + tc, :] = rx_ref[...].astype(F32)

    hx = cb_ref[...] + cw_ref[CONV_WIDTH - 1:CONV_WIDTH, :] * xbuf_ref[SUBLANES:SUBLANES + tc, :]
    for j in range(CONV_WIDTH - 1):
        start = SUBLANES - (CONV_WIDTH - 1) + j
        hx = hx + cw_ref[j:j + 1, :] * xbuf_ref[start:start + tc, :]

    lam = lam_ref[...]
    neg_c_softplus = -LRU_C * (jnp.maximum(-lam, 0.0) + jnp.log1p(jnp.exp(-jnp.abs(lam))))
    for gi in range(gb):
        cs = slice(gi * LRU_BLOCK, (gi + 1) * LRU_BLOCK)
        hxg = hx[:, cs]
        hxb = hxg.astype(BF16)
        r = jax.nn.sigmoid(_dot(hxb, wa_ref[gi]) + ba_ref[:, cs])
        i = jax.nn.sigmoid(_dot(hxb, wx_ref[gi]) + bx_ref[:, cs])
        log_a = neg_c_softplus[:, cs] * r
        a_ref[:, cs] = jnp.exp(log_a)
        u_ref[:, cs] = jnp.sqrt(1.0 - jnp.exp(2.0 * log_a)) * (i * hxg)

    row = lax.broadcasted_iota(jnp.int32, (SUBLANES, wc), 0)

    def body(s, carry):
        r0 = pl.multiple_of(s * SUBLANES, SUBLANES)
        a8 = a_ref[pl.ds(r0, SUBLANES), :]
        u8 = u_ref[pl.ds(r0, SUBLANES), :]
        for d in (1, 2, 4):
            keep = row >= d
            a_sh = pltpu.roll(a8, d, 0)
            u_sh = pltpu.roll(u8, d, 0)
            u8 = jnp.where(keep, a8 * u_sh + u8, u8)
            a8 = jnp.where(keep, a8 * a_sh, a8)
        h8 = u8 + a8 * carry
        u_ref[pl.ds(r0, SUBLANES), :] = h8
        return h8[SUBLANES - 1:SUBLANES, :]

    hc_ref[...] = lax.fori_loop(0, tc // SUBLANES, body, hc_ref[...], unroll=4)

    o_ref[...] = (u_ref[...] * jax.nn.gelu(ry_ref[...].astype(F32))).astype(BF16)


def _lru(proj, cw, cb, wa, ba, wx, bx, lam, batch, seq, tc, wc, off):
    m = proj.shape[0]
    width = cw.shape[1]
    gb = wc // LRU_BLOCK
    nt = seq // tc
    rowmap = lambda b, n, t: b * nt + t
    vec = lambda: pl.BlockSpec((1, wc), lambda b, n, t: (0, n))
    return pl.pallas_call(
        _lru_kernel,
        grid=(batch, width // wc, nt),
        in_specs=[
            pl.BlockSpec((tc, wc), lambda b, n, t: (rowmap(b, n, t), off["rx"] // wc + n)),
            pl.BlockSpec((tc, wc), lambda b, n, t: (rowmap(b, n, t), off["ry"] // wc + n)),
            pl.BlockSpec((CONV_WIDTH, wc), lambda b, n, t: (0, n)),
            vec(),
            pl.BlockSpec((gb, LRU_BLOCK, LRU_BLOCK), lambda b, n, t: (n, 0, 0)),
            vec(),
            pl.BlockSpec((gb, LRU_BLOCK, LRU_BLOCK), lambda b, n, t: (n, 0, 0)),
            vec(),
            vec(),
        ],
        out_specs=pl.BlockSpec((tc, wc), lambda b, n, t: (rowmap(b, n, t), n)),
        out_shape=jax.ShapeDtypeStruct((m, width), BF16),
        scratch_shapes=[
            pltpu.VMEM((tc + SUBLANES, wc), F32),
            pltpu.VMEM((tc, wc), F32),
            pltpu.VMEM((tc, wc), F32),
            pltpu.VMEM((1, wc), F32),
        ],
        compiler_params=_cparams(("parallel", "parallel", "arbitrary")),
        name="lru",
    )(proj, proj, cw, cb, wa, ba, wx, bx, lam)


def _post_kernel(za_ref, zb_ref, ga_ref, gb_ref, h_ref, woa_ref, wob_ref, wout_ref, lg_ref, lb_ref, o_ref,
                 *, alpha):
    ya = _dot(za_ref[...], woa_ref[...])
    yb = _dot(zb_ref[...], wob_ref[...])
    merged = (jax.nn.sigmoid(ga_ref[...].astype(F32)) * ya
              + jax.nn.sigmoid(gb_ref[...].astype(F32)) * yb)
    mix = _dot(merged.astype(BF16), wout_ref[...])
    o_ref[...] = _layer_norm(alpha * h_ref[...] + mix, lg_ref[...], lb_ref[...])


def _post(za, zb, proj, h, woa, wob, wout, lg, lb, tm, off, alpha):
    m, d = h.shape
    resident = lambda shape: pl.BlockSpec(shape, lambda i: (0, 0), pipeline_mode=pl.Buffered(1))
    return pl.pallas_call(
        functools.partial(_post_kernel, alpha=alpha),
        grid=(m // tm,),
        in_specs=[
            pl.BlockSpec((tm, d), lambda i: (i, 0)),
            pl.BlockSpec((tm, d), lambda i: (i, 0)),
            pl.BlockSpec((tm, d), lambda i: (i, off["ga"] // d)),
            pl.BlockSpec((tm, d), lambda i: (i, off["gb"] // d)),
            pl.BlockSpec((tm, d), lambda i: (i, 0)),
            resident(woa.shape),
            resident(wob.shape),
            resident(wout.shape),
            resident((1, d)),
            resident((1, d)),
        ],
        out_specs=pl.BlockSpec((tm, d), lambda i: (i, 0)),
        out_shape=jax.ShapeDtypeStruct((m, d), F32),
        compiler_params=_cparams(("parallel",)),
        name="post",
    )(za, zb, proj, proj, h, woa, wob, wout, lg, lb)


def _ffn_kernel(h_ref, wg_ref, wu_ref, wd_ref, lg_ref, lb_ref, o_ref, xb_ref, *, alpha):
    j = pl.program_id(1)

    @pl.when(j == 0)
    def _():
        xb_ref[...] = h_ref[...].astype(BF16)
        o_ref[...] = jnp.zeros_like(o_ref)

    xb = xb_ref[...]
    gate = _dot(xb, wg_ref[...])
    up = _dot(xb, wu_ref[...])
    act = (gate * jax.nn.sigmoid(gate) * up).astype(BF16)
    o_ref[...] += _dot(act, wd_ref[...])

    @pl.when(j == pl.num_programs(1) - 1)
    def _():
        o_ref[...] = _layer_norm(alpha * h_ref[...] + o_ref[...], lg_ref[...], lb_ref[...])


def _ffn(h, wgu, wd, lg, lb, tm, tf, alpha):
    m, d = h.shape
    dff = wd.shape[0]
    nf = dff // tf
    return pl.pallas_call(
        functools.partial(_ffn_kernel, alpha=alpha),
        grid=(m // tm, nf),
        in_specs=[
            pl.BlockSpec((tm, d), lambda i, j: (i, 0)),
            pl.BlockSpec((d, tf), lambda i, j: (0, j)),
            pl.BlockSpec((d, tf), lambda i, j: (0, nf + j)),
            pl.BlockSpec((tf, d), lambda i, j: (j, 0)),
            pl.BlockSpec((1, d), lambda i, j: (0, 0)),
            pl.BlockSpec((1, d), lambda i, j: (0, 0)),
        ],
        out_specs=pl.BlockSpec((tm, d), lambda i, j: (i, 0)),
        out_shape=jax.ShapeDtypeStruct((m, d), F32),
        scratch_shapes=[pltpu.VMEM((tm, d), BF16)],
        compiler_params=_cparams(("parallel", "arbitrary")),
        name="ffn",
    )(h, wgu, wgu, wd, lg, lb)


def _largest_tile(n, cap, quantum):
    t = min(cap, n) // quantum * quantum
    while n % t:
        t -= quantum
    return t


def kernel(x, w_in, gla_w_alpha, gla_b_alpha, gla_norm_g, gla_w_o, conv_w, conv_b, lru_w_a, lru_b_a, lru_w_x,
           lru_b_x, lru_lambda, rnn_w_o, w_out, ln1_g, ln1_b, w_gate_up, w_down, ln2_g, ln2_b):
    batch, seq, d = x.shape
    depth = w_in.shape[0]
    m = batch * seq
    key_w = gla_w_alpha.shape[2]
    val_w = gla_w_o.shape[1]
    lru_w = conv_w.shape[2]
    dff = w_down.shape[1]
    alpha = (2.0 * depth) ** 0.25

    lr0 = 2 * key_w + 2 * val_w
    off = {"q": 0, "k": key_w, "v": 2 * key_w, "g": 2 * key_w + val_w,
           "rx": lr0, "ry": lr0 + lru_w, "ga": lr0 + 2 * lru_w, "gb": lr0 + 2 * lru_w + d}
    n_main = lr0 + 2 * lru_w + 2 * d

    tm_proj = _largest_tile(seq, 1024, 256)
    tn_proj = _largest_tile(n_main, 1024, 512)
    tc_gla = _largest_tile(seq, 512, GLA_CHUNK)
    tc_lru = _largest_tile(seq, 512, SUBLANES)
    tm_post = _largest_tile(seq, 256, 128)
    tm_ffn = _largest_tile(seq, 512, 256)
    tf_ffn = _largest_tile(dff, 512, 256)

    h = x.reshape(m, d)
    for l in range(depth):
        wl = w_in[l]
        w_main = jnp.concatenate([wl[:, :lr0], wl[:, lr0 + GLA_LOW_RANK:]], axis=1).astype(BF16)
        w_lr = jnp.pad(wl[:, lr0:lr0 + GLA_LOW_RANK], ((0, 0), (0, LANES - GLA_LOW_RANK))).astype(BF16)
        w_alpha = jnp.pad(gla_w_alpha[l], ((0, LANES - GLA_LOW_RANK), (0, 0))).astype(BF16)

        proj, lr = _in_proj(h, w_main, w_lr, tm_proj, tn_proj)
        za = _gla(proj, lr, w_alpha, gla_b_alpha[l][None, :], gla_norm_g[l][None, :], batch, seq, tc_gla, off)
        zb = _lru(proj, conv_w[l], conv_b[l][None, :], lru_w_a[l].astype(BF16), lru_b_a[l][None, :],
                  lru_w_x[l].astype(BF16), lru_b_x[l][None, :], lru_lambda[l][None, :],
                  batch, seq, tc_lru, 2 * LRU_BLOCK, off)
        h = _post(za, zb, proj, h, gla_w_o[l].astype(BF16), rnn_w_o[l].astype(BF16), w_out[l].astype(BF16),
                  ln1_g[l][None, :], ln1_b[l][None, :], tm_post, off, alpha)
        h = _ffn(h, w_gate_up[l].astype(BF16), w_down[l].astype(BF16), ln2_g[l][None, :], ln2_b[l][None, :],
                 tm_ffn, tf_ffn, alpha)
    return h.reshape(batch, seq, d)
```

```python
import functools

import jax
import jax.numpy as jnp
from jax import lax
from jax.experimental import pallas as pl
from jax.experimental.pallas import tpu as pltpu

F32 = jnp.float32
BF16 = jnp.bfloat16

GLA_HEADS = 4
GLA_LOW_RANK = 16
GLA_TAU = 16.0
GLA_CHUNK = 64
GLA_NORM_EPS = 1e-5
LRU_BLOCK = 256
LRU_C = 8.0
CONV_WIDTH = 4
LN_EPS = 1e-5
GELU_C0 = 0.7978845608028654
GELU_C1 = 0.044715 * GELU_C0

LANES = 128
SUBLANES = 8
VMEM_LIMIT_BYTES = 56 * 1024 * 1024
F32_TINY = 1e-30


def _cparams(sem):
    return pltpu.CompilerParams(dimension_semantics=sem, vmem_limit_bytes=VMEM_LIMIT_BYTES)


def _dot(a, b):
    return jnp.dot(a, b, preferred_element_type=F32)


def _dot_nt(a, b):
    return lax.dot_general(a, b, (((1,), (1,)), ((), ())), preferred_element_type=F32)


def _dot_tn(a, b):
    return lax.dot_general(a, b, (((0,), (0,)), ((), ())), preferred_element_type=F32)


def _layer_norm(z, g, b):
    mu = jnp.mean(z, axis=-1, keepdims=True)
    zc = z - mu
    var = jnp.mean(zc * zc, axis=-1, keepdims=True)
    return zc * lax.rsqrt(var + LN_EPS) * g + b


def _in_proj_kernel(x_ref, w_ref, wlr_ref, o_ref, lr_ref, xb_ref):
    @pl.when(pl.program_id(1) == 0)
    def _():
        xb = x_ref[...].astype(BF16)
        xb_ref[...] = xb
        lr_ref[...] = _dot(xb, wlr_ref[...]).astype(BF16)

    o_ref[...] = _dot(xb_ref[...], w_ref[...]).astype(BF16)


def _in_proj(x, w, wlr, tm, tn):
    m, d = x.shape
    n = w.shape[1]
    return pl.pallas_call(
        _in_proj_kernel,
        grid=(m // tm, n // tn),
        in_specs=[
            pl.BlockSpec((tm, d), lambda i, j: (i, 0)),
            pl.BlockSpec((d, tn), lambda i, j: (0, j)),
            pl.BlockSpec((d, LANES), lambda i, j: (0, 0)),
        ],
        out_specs=[
            pl.BlockSpec((tm, tn), lambda i, j: (i, j)),
            pl.BlockSpec((tm, LANES), lambda i, j: (i, 0)),
        ],
        out_shape=[jax.ShapeDtypeStruct((m, n), BF16), jax.ShapeDtypeStruct((m, LANES), BF16)],
        scratch_shapes=[pltpu.VMEM((tm, d), BF16)],
        compiler_params=_cparams(("parallel", "arbitrary")),
        name="in_proj",
    )(x, w, wlr)


def _gla_kernel(q_ref, k_ref, v_ref, g_ref, lr_ref, wa_ref, ba_ref, ng_ref, o_ref, st_ref, *, scale):
    tc = q_ref.shape[0]
    c = GLA_CHUNK

    @pl.when(pl.program_id(2) == 0)
    def _():
        st_ref[...] = jnp.zeros_like(st_ref)

    alpha_pre = _dot(lr_ref[...], wa_ref[...]) + ba_ref[...]
    log_a = (jnp.minimum(alpha_pre, 0.0) - jnp.log1p(jnp.exp(-jnp.abs(alpha_pre)))) * (1.0 / GLA_TAU)

    row = lax.broadcasted_iota(jnp.int32, (c, c), 0)
    col = lax.broadcasted_iota(jnp.int32, (c, c), 1)
    causal = row >= col
    tri = jnp.where(causal, 1.0, 0.0).astype(BF16)

    for ci in range(tc // c):
        sl = slice(ci * c, (ci + 1) * c)
        la = log_a[sl]
        hi = la.astype(BF16)
        r1 = la - hi.astype(F32)
        mid = r1.astype(BF16)
        lo = (r1 - mid.astype(F32)).astype(BF16)
        bc = _dot(tri, hi) + _dot(tri, mid) + _dot(tri, lo)
        b_last = bc[c - 1:c, :]

        qc = q_ref[sl, :].astype(F32) * scale
        kc = k_ref[sl, :].astype(F32)
        vc = v_ref[sl, :]
        q_dec = (qc * jnp.exp(bc)).astype(BF16)
        k_inv = (kc * jnp.exp(-bc)).astype(BF16)
        k_end = (kc * jnp.exp(b_last - bc)).astype(BF16)

        scores = jnp.where(causal, _dot_nt(q_dec, k_inv), 0.0).astype(BF16)
        o_intra = _dot(scores, vc)
        st = st_ref[...]
        o_inter = _dot_nt(q_dec, st.astype(BF16))
        st_ref[...] = st * jnp.exp(b_last) + _dot_tn(vc, k_end)

        o = o_intra + o_inter
        ms = jnp.mean(o * o, axis=-1, keepdims=True)
        y = o * lax.rsqrt(ms + GLA_NORM_EPS) * ng_ref[...]
        g = g_ref[sl, :].astype(F32)
        o_ref[sl, :] = (y * (g * jax.nn.sigmoid(g))).astype(BF16)


def _gla(proj, lr, wa, ba, ng, layer, batch, seq, tc, off):
    m = proj.shape[0]
    dk = wa.shape[2] // GLA_HEADS
    dv = ng.shape[2]
    nt = seq // tc
    rowmap = lambda b, h, t: b * nt + t
    return pl.pallas_call(
        functools.partial(_gla_kernel, scale=float(dk) ** -0.5),
        grid=(batch, GLA_HEADS, nt),
        in_specs=[
            pl.BlockSpec((tc, dk), lambda b, h, t: (rowmap(b, h, t), off["q"] // dk + h)),
            pl.BlockSpec((tc, dk), lambda b, h, t: (rowmap(b, h, t), off["k"] // dk + h)),
            pl.BlockSpec((tc, dv), lambda b, h, t: (rowmap(b, h, t), off["v"] // dv + h)),
            pl.BlockSpec((tc, dv), lambda b, h, t: (rowmap(b, h, t), off["g"] // dv + h)),
            pl.BlockSpec((tc, LANES), lambda b, h, t: (rowmap(b, h, t), 0)),
            pl.BlockSpec((None, LANES, dk), lambda b, h, t: (layer, 0, h)),
            pl.BlockSpec((None, 1, dk), lambda b, h, t: (layer, 0, h)),
            pl.BlockSpec((None, 1, dv), lambda b, h, t: (layer, 0, 0)),
        ],
        out_specs=pl.BlockSpec((tc, dv), lambda b, h, t: (rowmap(b, h, t), h)),
        out_shape=jax.ShapeDtypeStruct((m, GLA_HEADS * dv), BF16),
        scratch_shapes=[pltpu.VMEM((dv, dk), F32)],
        compiler_params=_cparams(("parallel", "parallel", "arbitrary")),
        name="gla",
    )(proj, proj, proj, proj, lr, wa, ba, ng)


def _lru_kernel(rx_ref, ry_ref, cw_ref, cb_ref, wa_ref, ba_ref, wx_ref, bx_ref, lam_ref, o_ref,
                xbuf_ref, a_ref, u_ref, hc_ref):
    tc, wc = rx_ref.shape
    gb = wc // LRU_BLOCK
    t = pl.program_id(2)

    @pl.when(t == 0)
    def _():
        xbuf_ref[0:SUBLANES, :] = jnp.zeros((SUBLANES, wc), F32)
        hc_ref[...] = jnp.zeros_like(hc_ref)

    @pl.when(t > 0)
    def _():
        xbuf_ref[0:SUBLANES, :] = xbuf_ref[tc:tc + SUBLANES, :]

    xbuf_ref[SUBLANES:SUBLANES + tc, :] = rx_ref[...].astype(F32)

    hx = cb_ref[...] + cw_ref[CONV_WIDTH - 1:CONV_WIDTH, :] * xbuf_ref[SUBLANES:SUBLANES + tc, :]
    for j in range(CONV_WIDTH - 1):
        start = SUBLANES - (CONV_WIDTH - 1) + j
        hx = hx + cw_ref[j:j + 1, :] * xbuf_ref[start:start + tc, :]

    lam = lam_ref[...]
    neg_c_softplus = -LRU_C * (jnp.maximum(-lam, 0.0) + jnp.log1p(jnp.exp(-jnp.abs(lam))))
    for gi in range(gb):
        cs = slice(gi * LRU_BLOCK, (gi + 1) * LRU_BLOCK)
        hxg = hx[:, cs]
        hxb = hxg.astype(BF16)
        r = jax.nn.sigmoid(_dot(hxb, wa_ref[gi]) + ba_ref[:, cs])
        i = jax.nn.sigmoid(_dot(hxb, wx_ref[gi]) + bx_ref[:, cs])
        a = jnp.exp(neg_c_softplus[:, cs] * r)
        y = 1.0 - a * a
        a_ref[:, cs] = a
        u_ref[:, cs] = (y * lax.rsqrt(jnp.maximum(y, F32_TINY))) * (i * hxg)

    row = lax.broadcasted_iota(jnp.int32, (SUBLANES, wc), 0)

    def body(s, carry):
        r0 = pl.multiple_of(s * SUBLANES, SUBLANES)
        a8 = a_ref[pl.ds(r0, SUBLANES), :]
        u8 = u_ref[pl.ds(r0, SUBLANES), :]
        for d in (1, 2, 4):
            keep = row >= d
            a_sh = pltpu.roll(a8, d, 0)
            u_sh = pltpu.roll(u8, d, 0)
            u8 = jnp.where(keep, a8 * u_sh + u8, u8)
            a8 = jnp.where(keep, a8 * a_sh, a8)
        h8 = u8 + a8 * carry
        u_ref[pl.ds(r0, SUBLANES), :] = h8
        return h8[SUBLANES - 1:SUBLANES, :]

    hc_ref[...] = lax.fori_loop(0, tc // SUBLANES, body, hc_ref[...], unroll=4)

    ry = ry_ref[...].astype(F32)
    th = jnp.tanh(ry * (GELU_C0 + GELU_C1 * (ry * ry)))
    o_ref[...] = (u_ref[...] * (0.5 * ry) * (1.0 + th)).astype(BF16)


def _lru(proj, cw, cb, wa, ba, wx, bx, lam, layer, batch, seq, tc, wc, off):
    m = proj.shape[0]
    width = cw.shape[2]
    gb = wc // LRU_BLOCK
    nt = seq // tc
    rowmap = lambda b, n, t: b * nt + t
    vec = lambda: pl.BlockSpec((None, 1, wc), lambda b, n, t: (layer, 0, n))
    gate_w = lambda: pl.BlockSpec((None, gb, LRU_BLOCK, LRU_BLOCK), lambda b, n, t: (layer, n, 0, 0))
    return pl.pallas_call(
        _lru_kernel,
        grid=(batch, width // wc, nt),
        in_specs=[
            pl.BlockSpec((tc, wc), lambda b, n, t: (rowmap(b, n, t), off["rx"] // wc + n)),
            pl.BlockSpec((tc, wc), lambda b, n, t: (rowmap(b, n, t), off["ry"] // wc + n)),
            pl.BlockSpec((None, CONV_WIDTH, wc), lambda b, n, t: (layer, 0, n)),
            vec(), gate_w(), vec(), gate_w(), vec(), vec(),
        ],
        out_specs=pl.BlockSpec((tc, wc), lambda b, n, t: (rowmap(b, n, t), n)),
        out_shape=jax.ShapeDtypeStruct((m, width), BF16),
        scratch_shapes=[
            pltpu.VMEM((tc + SUBLANES, wc), F32),
            pltpu.VMEM((tc, wc), F32),
            pltpu.VMEM((tc, wc), F32),
            pltpu.VMEM((1, wc), F32),
        ],
        compiler_params=_cparams(("parallel", "parallel", "arbitrary")),
        name="lru",
    )(proj, proj, cw, cb, wa, ba, wx, bx, lam)


def _post_kernel(za_ref, zb_ref, ga_ref, gb_ref, h_ref, woa_ref, wob_ref, wout_ref, lg_ref, lb_ref,
                 o_ref, ob_ref, wa_s, wb_s, wo_s, *, alpha, npre):
    s = pl.program_id(0)
    rc = woa_ref.shape[0]

    @pl.when(s < npre)
    def _():
        r0 = pl.multiple_of(s * rc, rc)
        wa_s[pl.ds(r0, rc), :] = woa_ref[...].astype(BF16)
        wb_s[pl.ds(r0, rc), :] = wob_ref[...].astype(BF16)
        wo_s[pl.ds(r0, rc), :] = wout_ref[...].astype(BF16)

    @pl.when(s >= npre)
    def _():
        ya = _dot(za_ref[...], wa_s[...])
        yb = _dot(zb_ref[...], wb_s[...])
        merged = (jax.nn.sigmoid(ga_ref[...].astype(F32)) * ya
                  + jax.nn.sigmoid(gb_ref[...].astype(F32)) * yb)
        mix = _dot(merged.astype(BF16), wo_s[...])
        h1 = _layer_norm(alpha * h_ref[...] + mix, lg_ref[...], lb_ref[...])
        o_ref[...] = h1
        ob_ref[...] = h1.astype(BF16)


def _post(za, zb, proj, h, woa, wob, wout, lg, lb, layer, tm, rc, off, alpha):
    m, d = h.shape
    npre = d // rc
    tile = lambda s: jnp.maximum(s - npre, 0)
    chunk = lambda s: jnp.minimum(s, npre - 1)
    act = lambda col: pl.BlockSpec((tm, d), lambda s: (tile(s), col))
    wchunk = lambda: pl.BlockSpec((None, rc, d), lambda s: (layer, chunk(s), 0))
    vec = lambda: pl.BlockSpec((None, 1, d), lambda s: (layer, 0, 0))
    return pl.pallas_call(
        functools.partial(_post_kernel, alpha=alpha, npre=npre),
        grid=(npre + m // tm,),
        in_specs=[act(0), act(0), act(off["ga"] // d), act(off["gb"] // d), act(0),
                  wchunk(), wchunk(), wchunk(), vec(), vec()],
        out_specs=[act(0), act(0)],
        out_shape=[jax.ShapeDtypeStruct((m, d), F32), jax.ShapeDtypeStruct((m, d), BF16)],
        scratch_shapes=[pltpu.VMEM((d, d), BF16)] * 3,
        compiler_params=_cparams(("arbitrary",)),
        name="post",
    )(za, zb, proj, proj, h, woa, wob, wout, lg, lb)


def _gate_up_kernel(x_ref, wg_ref, wu_ref, o_ref, wg_s, wu_s):
    @pl.when(pl.program_id(1) == 0)
    def _():
        wg_s[...] = wg_ref[...].astype(BF16)
        wu_s[...] = wu_ref[...].astype(BF16)

    x = x_ref[...]
    gate = _dot(x, wg_s[...])
    up = _dot(x, wu_s[...])
    o_ref[...] = (gate * jax.nn.sigmoid(gate) * up).astype(BF16)


def _gate_up(xb, wgu, layer, tm, tf):
    m, d = xb.shape
    dff = wgu.shape[2] // 2
    nf = dff // tf
    return pl.pallas_call(
        _gate_up_kernel,
        grid=(nf, m // tm),
        in_specs=[
            pl.BlockSpec((tm, d), lambda j, i: (i, 0)),
            pl.BlockSpec((None, d, tf), lambda j, i: (layer, 0, j)),
            pl.BlockSpec((None, d, tf), lambda j, i: (layer, 0, nf + j)),
        ],
        out_specs=pl.BlockSpec((tm, tf), lambda j, i: (i, j)),
        out_shape=jax.ShapeDtypeStruct((m, dff), BF16),
        scratch_shapes=[pltpu.VMEM((d, tf), BF16)] * 2,
        compiler_params=_cparams(("arbitrary", "arbitrary")),
        name="gate_up",
    )(xb, wgu, wgu)


def _down_kernel(a_ref, h_ref, wd_ref, lg_ref, lb_ref, o_ref, ob_ref, wd_s, *, alpha, npre):
    s = pl.program_id(0)
    rc = wd_ref.shape[0]

    @pl.when(s < npre)
    def _():
        r0 = pl.multiple_of(s * rc, rc)
        wd_s[pl.ds(r0, rc), :] = wd_ref[...].astype(BF16)

    @pl.when(s >= npre)
    def _():
        y = _dot(a_ref[...], wd_s[...])
        h2 = _layer_norm(alpha * h_ref[...] + y, lg_ref[...], lb_ref[...])
        o_ref[...] = h2
        ob_ref[...] = h2.astype(BF16)


def _down(act, h, wd, lg, lb, layer, tm, rc, alpha):
    m, d = h.shape
    dff = wd.shape[1]
    npre = dff // rc
    tile = lambda s: jnp.maximum(s - npre, 0)
    chunk = lambda s: jnp.minimum(s, npre - 1)
    return pl.pallas_call(
        functools.partial(_down_kernel, alpha=alpha, npre=npre),
        grid=(npre + m // tm,),
        in_specs=[
            pl.BlockSpec((tm, dff), lambda s: (tile(s), 0)),
            pl.BlockSpec((tm, d), lambda s: (tile(s), 0)),
            pl.BlockSpec((None, rc, d), lambda s: (layer, chunk(s), 0)),
            pl.BlockSpec((None, 1, d), lambda s: (layer, 0, 0)),
            pl.BlockSpec((None, 1, d), lambda s: (layer, 0, 0)),
        ],
        out_specs=[pl.BlockSpec((tm, d), lambda s: (tile(s), 0)),
                   pl.BlockSpec((tm, d), lambda s: (tile(s), 0))],
        out_shape=[jax.ShapeDtypeStruct((m, d), F32), jax.ShapeDtypeStruct((m, d), BF16)],
        scratch_shapes=[pltpu.VMEM((dff, d), BF16)],
        compiler_params=_cparams(("arbitrary",)),
        name="down",
    )(act, h, wd, lg, lb)


def _largest_tile(n, cap, quantum):
    t = min(cap, n) // quantum * quantum
    while n % t:
        t -= quantum
    return t


def kernel(x, w_in, gla_w_alpha, gla_b_alpha, gla_norm_g, gla_w_o, conv_w, conv_b, lru_w_a, lru_b_a, lru_w_x,
           lru_b_x, lru_lambda, rnn_w_o, w_out, ln1_g, ln1_b, w_gate_up, w_down, ln2_g, ln2_b):
    batch, seq, d = x.shape
    depth = w_in.shape[0]
    m = batch * seq
    key_w = gla_w_alpha.shape[2]
    val_w = gla_w_o.shape[1]
    lru_w = conv_w.shape[2]
    dff = w_down.shape[1]
    alpha = (2.0 * depth) ** 0.25

    lr0 = 2 * key_w + 2 * val_w
    off = {"q": 0, "k": key_w, "v": 2 * key_w, "g": 2 * key_w + val_w,
           "rx": lr0, "ry": lr0 + lru_w, "ga": lr0 + 2 * lru_w, "gb": lr0 + 2 * lru_w + d}
    n_main = lr0 + 2 * lru_w + 2 * d

    tm_proj = _largest_tile(seq, 1024, 256)
    tn_proj = _largest_tile(n_main, 2048, 512)
    tc_gla = _largest_tile(seq, 512, GLA_CHUNK)
    tc_lru = _largest_tile(seq, 512, SUBLANES)
    tm_post = _largest_tile(seq, 256, 128)
    rc_post = 128
    tm_gu = _largest_tile(m, 2048, 256)
    tf_gu = _largest_tile(dff, 512, 256)
    tm_down = _largest_tile(seq, 256, 128)
    rc_down = 256

    row3 = lambda p: p[:, None, :]
    w_alpha = jnp.pad(gla_w_alpha, ((0, 0), (0, LANES - GLA_LOW_RANK), (0, 0))).astype(BF16)
    b_alpha, norm_g = row3(gla_b_alpha), row3(gla_norm_g)
    cb, ba, bx, lam = row3(conv_b), row3(lru_b_a), row3(lru_b_x), row3(lru_lambda)
    wa_b, wx_b = lru_w_a.astype(BF16), lru_w_x.astype(BF16)
    l1g, l1b, l2g, l2b = row3(ln1_g), row3(ln1_b), row3(ln2_g), row3(ln2_b)

    h = x.reshape(m, d)
    hb = h
    for l in range(depth):
        wl = w_in[l]
        w_main = jnp.concatenate([wl[:, :lr0], wl[:, lr0 + GLA_LOW_RANK:]], axis=1).astype(BF16)
        w_lr = jnp.pad(wl[:, lr0:lr0 + GLA_LOW_RANK], ((0, 0), (0, LANES - GLA_LOW_RANK))).astype(BF16)

        proj, lr = _in_proj(hb, w_main, w_lr, tm_proj, tn_proj)
        za = _gla(proj, lr, w_alpha, b_alpha, norm_g, l, batch, seq, tc_gla, off)
        zb = _lru(proj, conv_w, cb, wa_b, ba, wx_b, bx, lam, l, batch, seq, tc_lru, 2 * LRU_BLOCK, off)
        h, hb = _post(za, zb, proj, h, gla_w_o, rnn_w_o, w_out, l1g, l1b, l, tm_post, rc_post, off, alpha)
        act = _gate_up(hb, w_gate_up, l, tm_gu, tf_gu)
        h, hb = _down(act, h, w_down, l2g, l2b, l, tm_down, rc_down, alpha)
    return h.reshape(batch, seq, d)
```

```python
import functools

import jax
import jax.numpy as jnp
from jax import lax
from jax.experimental import pallas as pl
from jax.experimental.pallas import tpu as pltpu

F32 = jnp.float32
BF16 = jnp.bfloat16

GLA_HEADS = 4
GLA_LOW_RANK = 16
GLA_TAU = 16.0
GLA_CHUNK = 64
GLA_BLOCK = 256
GLA_NORM_EPS = 1e-5
LRU_BLOCK = 256
LRU_C = 8.0
CONV_WIDTH = 4
LN_EPS = 1e-5
GELU_C0 = 0.7978845608028654
GELU_C1 = 0.044715 * GELU_C0

LANES = 128
SUBLANES = 8
VMEM_LIMIT_BYTES = 56 * 1024 * 1024
F32_TINY = 1e-30


def _cparams(sem):
    return pltpu.CompilerParams(dimension_semantics=sem, vmem_limit_bytes=VMEM_LIMIT_BYTES)


def _dot(a, b):
    return jnp.dot(a, b, preferred_element_type=F32)


def _dot_nt(a, b):
    return lax.dot_general(a, b, (((1,), (1,)), ((), ())), preferred_element_type=F32)


def _dot_tn(a, b):
    return lax.dot_general(a, b, (((0,), (0,)), ((), ())), preferred_element_type=F32)


def _layer_norm(z, g, b):
    mu = jnp.mean(z, axis=-1, keepdims=True)
    zc = z - mu
    var = jnp.mean(zc * zc, axis=-1, keepdims=True)
    return zc * lax.rsqrt(var + LN_EPS) * g + b


def _regroup_kernel(w_ref, o_ref, *, lr0, n_main):
    rt = w_ref.shape[0]
    w = w_ref[...]
    o_ref[:, :lr0] = w[:, :lr0].astype(BF16)
    o_ref[:, lr0:n_main] = w[:, lr0 + GLA_LOW_RANK:].astype(BF16)
    lr_cols = jnp.concatenate(
        [w[:, lr0:lr0 + GLA_LOW_RANK], jnp.zeros((rt, LANES - GLA_LOW_RANK), F32)], axis=1)
    o_ref[:, n_main:] = lr_cols.astype(BF16)


def _regroup(w_in, layer, lr0, rt):
    _, d, n_in = w_in.shape
    n_main = n_in - GLA_LOW_RANK
    return pl.pallas_call(
        functools.partial(_regroup_kernel, lr0=lr0, n_main=n_main),
        grid=(d // rt,),
        in_specs=[pl.BlockSpec((None, rt, n_in), lambda i: (layer, i, 0))],
        out_specs=pl.BlockSpec((rt, n_main + LANES), lambda i: (i, 0)),
        out_shape=jax.ShapeDtypeStruct((d, n_main + LANES), BF16),
        compiler_params=_cparams(("parallel",)),
        name="regroup",
    )(w_in)


def _in_proj_kernel(x_ref, w_ref, wlr_ref, o_ref, lr_ref, xb_ref):
    @pl.when(pl.program_id(1) == 0)
    def _():
        xb = x_ref[...].astype(BF16)
        xb_ref[...] = xb
        lr_ref[...] = _dot(xb, wlr_ref[...]).astype(BF16)

    o_ref[...] = _dot(xb_ref[...], w_ref[...]).astype(BF16)


def _in_proj(x, w, tm, tn):
    m, d = x.shape
    n = w.shape[1] - LANES
    return pl.pallas_call(
        _in_proj_kernel,
        grid=(m // tm, n // tn),
        in_specs=[
            pl.BlockSpec((tm, d), lambda i, j: (i, 0)),
            pl.BlockSpec((d, tn), lambda i, j: (0, j)),
            pl.BlockSpec((d, LANES), lambda i, j: (0, n // LANES)),
        ],
        out_specs=[
            pl.BlockSpec((tm, tn), lambda i, j: (i, j)),
            pl.BlockSpec((tm, LANES), lambda i, j: (i, 0)),
        ],
        out_shape=[jax.ShapeDtypeStruct((m, n), BF16), jax.ShapeDtypeStruct((m, LANES), BF16)],
        scratch_shapes=[pltpu.VMEM((tm, d), BF16)],
        compiler_params=_cparams(("parallel", "arbitrary")),
        name="in_proj",
    )(x, w, w)


def _gla_kernel(q_ref, k_ref, v_ref, g_ref, lr_ref, wa_ref, ba_ref, ng_ref, o_ref, s_ref, *, scale):
    tc, dk = q_ref.shape
    dv = v_ref.shape[1]
    c, blk = GLA_CHUNK, GLA_BLOCK
    nc = blk // c

    @pl.when(pl.program_id(2) == 0)
    def _():
        s_ref[...] = jnp.zeros_like(s_ref)

    alpha_pre = _dot(lr_ref[...], wa_ref[...]) + ba_ref[...]
    log_a = (jnp.minimum(alpha_pre, 0.0) - jnp.log1p(jnp.exp(-jnp.abs(alpha_pre)))) * (1.0 / GLA_TAU)

    row = lax.broadcasted_iota(jnp.int32, (blk, blk), 0)
    col = lax.broadcasted_iota(jnp.int32, (blk, blk), 1)
    lower = jnp.where(row >= col, 1.0, 0.0)
    tri = jnp.where((row // c) == (col // c), lower, 0.0).astype(BF16)
    zeros_chunk = jnp.zeros((c, dk), BF16)

    def rowsum(vs):
        out = None
        for v in vs:
            out = v if out is None else out + v
        return out

    def scaled(x, e):
        return x if e is None else x * jnp.exp(e)

    for bi in range(tc // blk):
        bs = slice(bi * blk, (bi + 1) * blk)
        la = log_a[bs]
        hi = la.astype(BF16)
        mid = (la - hi.astype(F32)).astype(BF16)
        r = _dot(tri, jnp.concatenate([hi, mid], axis=1))
        bc = r[:, :dk] + r[:, dk:]
        qf = q_ref[bs, :].astype(F32) * scale
        kf = k_ref[bs, :].astype(F32)
        vb = v_ref[bs, :]
        q_dec = qf * jnp.exp(bc)
        k_inv = (kf * jnp.exp(-bc)).astype(BF16)

        cs = [slice(j * c, (j + 1) * c) for j in range(nc)]
        bl = [bc[j * c + c - 1:j * c + c, :] for j in range(nc)]
        k_end = [kf[cs[j]] * jnp.exp(bl[j] - bc[cs[j]]) for j in range(nc)]

        strips = []
        for i in range(nc):
            rows = [scaled(k_end[j], rowsum(bl[j + 1:i])).astype(BF16) for j in range(i)]
            rows.append(k_inv[cs[i]])
            if (c * (i + 1)) % LANES:
                rows.append(zeros_chunk)
            keys_i = jnp.concatenate(rows, axis=0)
            w = keys_i.shape[0]
            sc = _dot_nt(q_dec[cs[i]].astype(BF16), keys_i)
            rr = lax.broadcasted_iota(jnp.int32, (c, w), 0)
            cc = lax.broadcasted_iota(jnp.int32, (c, w), 1)
            sc = jnp.where(cc <= rr + c * i, sc, 0.0).astype(BF16)
            if w < blk:
                sc = jnp.concatenate([sc, jnp.zeros((c, blk - w), BF16)], axis=1)
            strips.append(sc)
        p = jnp.concatenate(strips, axis=0)
        q_anch = jnp.concatenate(
            [scaled(q_dec[cs[i]], rowsum(bl[:i])).astype(BF16) for i in range(nc)], axis=0)
        k_anch = jnp.concatenate(
            [scaled(k_end[j], rowsum(bl[j + 1:])).astype(BF16) for j in range(nc)], axis=0)

        s = s_ref[...]
        o = _dot(jnp.concatenate([p, q_anch], axis=1),
                 jnp.concatenate([vb, s.astype(BF16)], axis=0))
        decay = jnp.exp(rowsum(bl))
        decay_t = jnp.transpose(jnp.broadcast_to(decay, (LANES, dk)))
        s_ref[...] = s * jnp.concatenate([decay_t] * (dv // LANES), axis=1) + _dot_tn(k_anch, vb)

        ms = jnp.mean(o * o, axis=-1, keepdims=True)
        y = o * lax.rsqrt(ms + GLA_NORM_EPS) * ng_ref[...]
        g = g_ref[bs, :].astype(F32)
        o_ref[bs, :] = (y * (g * jax.nn.sigmoid(g))).astype(BF16)


def _gla(proj, lr, wa, ba, ng, layer, batch, seq, tc, off):
    m = proj.shape[0]
    dk = wa.shape[2] // GLA_HEADS
    dv = ng.shape[2]
    nt = seq // tc
    rowmap = lambda b, h, t: b * nt + t
    return pl.pallas_call(
        functools.partial(_gla_kernel, scale=float(dk) ** -0.5),
        grid=(batch, GLA_HEADS, nt),
        in_specs=[
            pl.BlockSpec((tc, dk), lambda b, h, t: (rowmap(b, h, t), off["q"] // dk + h)),
            pl.BlockSpec((tc, dk), lambda b, h, t: (rowmap(b, h, t), off["k"] // dk + h)),
            pl.BlockSpec((tc, dv), lambda b, h, t: (rowmap(b, h, t), off["v"] // dv + h)),
            pl.BlockSpec((tc, dv), lambda b, h, t: (rowmap(b, h, t), off["g"] // dv + h)),
            pl.BlockSpec((tc, LANES), lambda b, h, t: (rowmap(b, h, t), 0)),
            pl.BlockSpec((None, LANES, dk), lambda b, h, t: (layer, 0, h)),
            pl.BlockSpec((None, 1, dk), lambda b, h, t: (layer, 0, h)),
            pl.BlockSpec((None, 1, dv), lambda b, h, t: (layer, 0, 0)),
        ],
        out_specs=pl.BlockSpec((tc, dv), lambda b, h, t: (rowmap(b, h, t), h)),
        out_shape=jax.ShapeDtypeStruct((m, GLA_HEADS * dv), BF16),
        scratch_shapes=[pltpu.VMEM((dk, dv), F32)],
        compiler_params=_cparams(("parallel", "parallel", "arbitrary")),
        name="gla",
    )(proj, proj, proj, proj, lr, wa, ba, ng)


def _lru_kernel(rx_ref, ry_ref, cw_ref, cb_ref, wa_ref, ba_ref, wx_ref, bx_ref, lam_ref, o_ref,
                xbuf_ref, a_ref, u_ref, hc_ref):
    tc, wc = rx_ref.shape
    gb = wc // LRU_BLOCK
    t = pl.program_id(2)

    @pl.when(t == 0)
    def _():
        xbuf_ref[0:SUBLANES, :] = jnp.zeros((SUBLANES, wc), F32)
        hc_ref[...] = jnp.zeros_like(hc_ref)

    @pl.when(t > 0)
    def _():
        xbuf_ref[0:SUBLANES, :] = xbuf_ref[tc:tc + SUBLANES, :]

    xbuf_ref[SUBLANES:SUBLANES + tc, :] = rx_ref[...].astype(F32)

    hx = cb_ref[...] + cw_ref[CONV_WIDTH - 1:CONV_WIDTH, :] * xbuf_ref[SUBLANES:SUBLANES + tc, :]
    for j in range(CONV_WIDTH - 1):
        start = SUBLANES - (CONV_WIDTH - 1) + j
        hx = hx + cw_ref[j:j + 1, :] * xbuf_ref[start:start + tc, :]

    lam = lam_ref[...]
    neg_c_softplus = -LRU_C * (jnp.maximum(-lam, 0.0) + jnp.log1p(jnp.exp(-jnp.abs(lam))))
    for gi in range(gb):
        cs = slice(gi * LRU_BLOCK, (gi + 1) * LRU_BLOCK)
        hxg = hx[:, cs]
        hxb = hxg.astype(BF16)
        r = jax.nn.sigmoid(_dot(hxb, wa_ref[gi]) + ba_ref[:, cs])
        i = jax.nn.sigmoid(_dot(hxb, wx_ref[gi]) + bx_ref[:, cs])
        a = jnp.exp(neg_c_softplus[:, cs] * r)
        y = 1.0 - a * a
        a_ref[:, cs] = a
        u_ref[:, cs] = (y * lax.rsqrt(jnp.maximum(y, F32_TINY))) * (i * hxg)

    row = lax.broadcasted_iota(jnp.int32, (SUBLANES, wc), 0)

    def body(s, carry):
        r0 = pl.multiple_of(s * SUBLANES, SUBLANES)
        a8 = a_ref[pl.ds(r0, SUBLANES), :]
        u8 = u_ref[pl.ds(r0, SUBLANES), :]
        for d in (1, 2, 4):
            keep = row >= d
            a_sh = pltpu.roll(a8, d, 0)
            u_sh = pltpu.roll(u8, d, 0)
            u8 = jnp.where(keep, a8 * u_sh + u8, u8)
            a8 = jnp.where(keep, a8 * a_sh, a8)
        h8 = u8 + a8 * carry
        u_ref[pl.ds(r0, SUBLANES), :] = h8
        return h8[SUBLANES - 1:SUBLANES, :]

    hc_ref[...] = lax.fori_loop(0, tc // SUBLANES, body, hc_ref[...], unroll=4)

    ry = ry_ref[...].astype(F32)
    th = jnp.tanh(ry * (GELU_C0 + GELU_C1 * (ry * ry)))
    o_ref[...] = (u_ref[...] * (0.5 * ry) * (1.0 + th)).astype(BF16)


def _lru(proj, cw, cb, wa, ba, wx, bx, lam, layer, batch, seq, tc, wc, off):
    m = proj.shape[0]
    width = cw.shape[2]
    gb = wc // LRU_BLOCK
    nt = seq // tc
    rowmap = lambda b, n, t: b * nt + t
    vec = lambda: pl.BlockSpec((None, 1, wc), lambda b, n, t: (layer, 0, n))
    gate_w = lambda: pl.BlockSpec((None, gb, LRU_BLOCK, LRU_BLOCK), lambda b, n, t: (layer, n, 0, 0))
    return pl.pallas_call(
        _lru_kernel,
        grid=(batch, width // wc, nt),
        in_specs=[
            pl.BlockSpec((tc, wc), lambda b, n, t: (rowmap(b, n, t), off["rx"] // wc + n)),
            pl.BlockSpec((tc, wc), lambda b, n, t: (rowmap(b, n, t), off["ry"] // wc + n)),
            pl.BlockSpec((None, CONV_WIDTH, wc), lambda b, n, t: (layer, 0, n)),
            vec(), gate_w(), vec(), gate_w(), vec(), vec(),
        ],
        out_specs=pl.BlockSpec((tc, wc), lambda b, n, t: (rowmap(b, n, t), n)),
        out_shape=jax.ShapeDtypeStruct((m, width), BF16),
        scratch_shapes=[
            pltpu.VMEM((tc + SUBLANES, wc), F32),
            pltpu.VMEM((tc, wc), F32),
            pltpu.VMEM((tc, wc), F32),
            pltpu.VMEM((1, wc), F32),
        ],
        compiler_params=_cparams(("parallel", "parallel", "arbitrary")),
        name="lru",
    )(proj, proj, cw, cb, wa, ba, wx, bx, lam)


def _post_kernel(za_ref, zb_ref, ga_ref, gb_ref, h_ref, woa_ref, wob_ref, wout_ref, lg_ref, lb_ref,
                 o_ref, ob_ref, wa_s, wb_s, wo_s, *, alpha, npre):
    s = pl.program_id(0)
    rc = woa_ref.shape[0]

    @pl.when(s < npre)
    def _():
        r0 = pl.multiple_of(s * rc, rc)
        wa_s[pl.ds(r0, rc), :] = woa_ref[...].astype(BF16)
        wb_s[pl.ds(r0, rc), :] = wob_ref[...].astype(BF16)
        wo_s[pl.ds(r0, rc), :] = wout_ref[...].astype(BF16)

    @pl.when(s >= npre)
    def _():
        ya = _dot(za_ref[...], wa_s[...])
        yb = _dot(zb_ref[...], wb_s[...])
        merged = (jax.nn.sigmoid(ga_ref[...].astype(F32)) * ya
                  + jax.nn.sigmoid(gb_ref[...].astype(F32)) * yb)
        mix = _dot(merged.astype(BF16), wo_s[...])
        h1 = _layer_norm(alpha * h_ref[...] + mix, lg_ref[...], lb_ref[...])
        o_ref[...] = h1
        ob_ref[...] = h1.astype(BF16)


def _post(za, zb, proj, h, woa, wob, wout, lg, lb, layer, tm, rc, off, alpha):
    m, d = h.shape
    npre = d // rc
    tile = lambda s: jnp.maximum(s - npre, 0)
    chunk = lambda s: jnp.minimum(s, npre - 1)
    act = lambda col: pl.BlockSpec((tm, d), lambda s: (tile(s), col))
    wchunk = lambda: pl.BlockSpec((None, rc, d), lambda s: (layer, chunk(s), 0))
    vec = lambda: pl.BlockSpec((None, 1, d), lambda s: (layer, 0, 0))
    return pl.pallas_call(
        functools.partial(_post_kernel, alpha=alpha, npre=npre),
        grid=(npre + m // tm,),
        in_specs=[act(0), act(0), act(off["ga"] // d), act(off["gb"] // d), act(0),
                  wchunk(), wchunk(), wchunk(), vec(), vec()],
        out_specs=[act(0), act(0)],
        out_shape=[jax.ShapeDtypeStruct((m, d), F32), jax.ShapeDtypeStruct((m, d), BF16)],
        scratch_shapes=[pltpu.VMEM((d, d), BF16)] * 3,
        compiler_params=_cparams(("arbitrary",)),
        name="post",
    )(za, zb, proj, proj, h, woa, wob, wout, lg, lb)


def _gate_up_kernel(x_ref, wg_ref, wu_ref, o_ref, w_s):
    tf = wg_ref.shape[1]

    @pl.when(pl.program_id(1) == 0)
    def _():
        w_s[:, :tf] = wg_ref[...].astype(BF16)
        w_s[:, tf:] = wu_ref[...].astype(BF16)

    r = _dot(x_ref[...], w_s[...])
    gate, up = r[:, :tf], r[:, tf:]
    o_ref[...] = (gate * jax.nn.sigmoid(gate) * up).astype(BF16)


def _gate_up(xb, wgu, layer, tm, tf):
    m, d = xb.shape
    dff = wgu.shape[2] // 2
    nf = dff // tf
    return pl.pallas_call(
        _gate_up_kernel,
        grid=(nf, m // tm),
        in_specs=[
            pl.BlockSpec((tm, d), lambda j, i: (i, 0)),
            pl.BlockSpec((None, d, tf), lambda j, i: (layer, 0, j)),
            pl.BlockSpec((None, d, tf), lambda j, i: (layer, 0, nf + j)),
        ],
        out_specs=pl.BlockSpec((tm, tf), lambda j, i: (i, j)),
        out_shape=jax.ShapeDtypeStruct((m, dff), BF16),
        scratch_shapes=[pltpu.VMEM((d, 2 * tf), BF16)],
        compiler_params=_cparams(("arbitrary", "arbitrary")),
        name="gate_up",
    )(xb, wgu, wgu)


def _down_kernel(a_ref, h_ref, wd_ref, lg_ref, lb_ref, o_ref, ob_ref, wd_s, *, alpha, npre):
    s = pl.program_id(0)
    rc = wd_ref.shape[0]

    @pl.when(s < npre)
    def _():
        r0 = pl.multiple_of(s * rc, rc)
        wd_s[pl.ds(r0, rc), :] = wd_ref[...].astype(BF16)

    @pl.when(s >= npre)
    def _():
        y = _dot(a_ref[...], wd_s[...])
        h2 = _layer_norm(alpha * h_ref[...] + y, lg_ref[...], lb_ref[...])
        o_ref[...] = h2
        ob_ref[...] = h2.astype(BF16)


def _down(act, h, wd, lg, lb, layer, tm, rc, alpha):
    m, d = h.shape
    dff = wd.shape[1]
    npre = dff // rc
    tile = lambda s: jnp.maximum(s - npre, 0)
    chunk = lambda s: jnp.minimum(s, npre - 1)
    return pl.pallas_call(
        functools.partial(_down_kernel, alpha=alpha, npre=npre),
        grid=(npre + m // tm,),
        in_specs=[
            pl.BlockSpec((tm, dff), lambda s: (tile(s), 0)),
            pl.BlockSpec((tm, d), lambda s: (tile(s), 0)),
            pl.BlockSpec((None, rc, d), lambda s: (layer, chunk(s), 0)),
            pl.BlockSpec((None, 1, d), lambda s: (layer, 0, 0)),
            pl.BlockSpec((None, 1, d), lambda s: (layer, 0, 0)),
        ],
        out_specs=[pl.BlockSpec((tm, d), lambda s: (tile(s), 0)),
                   pl.BlockSpec((tm, d), lambda s: (tile(s), 0))],
        out_shape=[jax.ShapeDtypeStruct((m, d), F32), jax.ShapeDtypeStruct((m, d), BF16)],
        scratch_shapes=[pltpu.VMEM((dff, d), BF16)],
        compiler_params=_cparams(("arbitrary",)),
        name="down",
    )(act, h, wd, lg, lb)


def _largest_tile(n, cap, quantum):
    t = min(cap, n) // quantum * quantum
    while n % t:
        t -= quantum
    return t


def kernel(x, w_in, gla_w_alpha, gla_b_alpha, gla_norm_g, gla_w_o, conv_w, conv_b, lru_w_a, lru_b_a, lru_w_x,
           lru_b_x, lru_lambda, rnn_w_o, w_out, ln1_g, ln1_b, w_gate_up, w_down, ln2_g, ln2_b):
    batch, seq, d = x.shape
    depth = w_in.shape[0]
    m = batch * seq
    key_w = gla_w_alpha.shape[2]
    val_w = gla_w_o.shape[1]
    lru_w = conv_w.shape[2]
    dff = w_down.shape[1]
    alpha = (2.0 * depth) ** 0.25

    lr0 = 2 * key_w + 2 * val_w
    off = {"q": 0, "k": key_w, "v": 2 * key_w, "g": 2 * key_w + val_w,
           "rx": lr0, "ry": lr0 + lru_w, "ga": lr0 + 2 * lru_w, "gb": lr0 + 2 * lru_w + d}
    n_main = lr0 + 2 * lru_w + 2 * d

    tm_proj = _largest_tile(seq, 1024, 256)
    tn_proj = _largest_tile(n_main, 2048, 512)
    tc_gla = _largest_tile(seq, 512, GLA_BLOCK)
    tc_lru = _largest_tile(seq, 512, SUBLANES)
    tm_post = _largest_tile(seq, 256, 128)
    rc_post = 128
    tm_gu = _largest_tile(m, 2048, 256)
    tf_gu = _largest_tile(dff, 512, 256)
    tm_down = _largest_tile(seq, 256, 128)
    rc_down = 256
    rt_regroup = _largest_tile(d, 128, SUBLANES)

    row3 = lambda p: p[:, None, :]
    w_alpha = jnp.pad(gla_w_alpha, ((0, 0), (0, LANES - GLA_LOW_RANK), (0, 0))).astype(BF16)
    b_alpha, norm_g = row3(gla_b_alpha), row3(gla_norm_g)
    cb, ba, bx, lam = row3(conv_b), row3(lru_b_a), row3(lru_b_x), row3(lru_lambda)
    wa_b, wx_b = lru_w_a.astype(BF16), lru_w_x.astype(BF16)
    l1g, l1b, l2g, l2b = row3(ln1_g), row3(ln1_b), row3(ln2_g), row3(ln2_b)

    h = x.reshape(m, d)
    hb = h
    for l in range(depth):
        proj, lr = _in_proj(hb, _regroup(w_in, l, lr0, rt_regroup), tm_proj, tn_proj)
        za = _gla(proj, lr, w_alpha, b_alpha, norm_g, l, batch, seq, tc_gla, off)
        zb = _lru(proj, conv_w, cb, wa_b, ba, wx_b, bx, lam, l, batch, seq, tc_lru, 4 * LRU_BLOCK, off)
        h, hb = _post(za, zb, proj, h, gla_w_o, rnn_w_o, w_out, l1g, l1b, l, tm_post, rc_post, off, alpha)
        act = _gate_up(hb, w_gate_up, l, tm_gu, tf_gu)
        h, hb = _down(act, h, w_down, l2g, l2b, l, tm_down, rc_down, alpha)
    return h.reshape(batch, seq, d)
```

```python
import functools

import jax
import jax.numpy as jnp
from jax import lax
from jax.experimental import pallas as pl
from jax.experimental.pallas import tpu as pltpu

F32 = jnp.float32
BF16 = jnp.bfloat16

GLA_HEADS = 4
GLA_LOW_RANK = 16
GLA_TAU = 16.0
GLA_CHUNK = 64
GLA_BLOCK = 256
GLA_NORM_EPS = 1e-5
LRU_BLOCK = 256
LRU_C = 8.0
CONV_WIDTH = 4
LN_EPS = 1e-5
GELU_C0 = 0.7978845608028654
GELU_C1 = 0.044715 * GELU_C0
GATE_UP_ROW_CHUNK = 512
RESIDENT_ROW_CHUNK = 128

LANES = 128
SUBLANES = 8
VMEM_LIMIT_BYTES = 56 * 1024 * 1024
F32_TINY = 1e-30


def _cparams(sem):
    return pltpu.CompilerParams(dimension_semantics=sem, vmem_limit_bytes=VMEM_LIMIT_BYTES)


def _dot(a, b):
    return jnp.dot(a, b, preferred_element_type=F32)


def _dot_nt(a, b):
    return lax.dot_general(a, b, (((1,), (1,)), ((), ())), preferred_element_type=F32)


def _dot_tn(a, b):
    return lax.dot_general(a, b, (((0,), (0,)), ((), ())), preferred_element_type=F32)


def _layer_norm(z, g, b):
    mu = jnp.mean(z, axis=-1, keepdims=True)
    zc = z - mu
    var = jnp.mean(zc * zc, axis=-1, keepdims=True)
    return zc * lax.rsqrt(var + LN_EPS) * g + b


def _regroup_kernel(a_ref, b_ref, c_ref, o_ref, lr_ref, *, gap_tile):
    j = pl.program_id(0)
    d = a_ref.shape[1]

    @pl.when(j < gap_tile)
    def _():
        o_ref[...] = a_ref[...].T.astype(BF16)

    @pl.when(j >= gap_tile)
    def _():
        src = jnp.concatenate([a_ref[GLA_LOW_RANK:, :], b_ref[...]], axis=0)
        o_ref[...] = src.T.astype(BF16)

    @pl.when(j == 0)
    def _():
        rows = jnp.concatenate([c_ref[...], jnp.zeros((LANES - GLA_LOW_RANK, d), F32)], axis=0)
        lr_ref[...] = rows.T.astype(BF16)


def _regroup(w_in_t, layer, lr0, ct):
    _, n_in, d = w_in_t.shape
    n_main = n_in - GLA_LOW_RANK
    assert lr0 % ct == 0 and n_main % ct == 0 and ct % GLA_LOW_RANK == 0
    sub = ct // GLA_LOW_RANK
    return pl.pallas_call(
        functools.partial(_regroup_kernel, gap_tile=lr0 // ct),
        grid=(n_main // ct,),
        in_specs=[
            pl.BlockSpec((None, ct, d), lambda j: (layer, j, 0)),
            pl.BlockSpec((None, GLA_LOW_RANK, d), lambda j: (layer, (j + 1) * sub, 0)),
            pl.BlockSpec((None, GLA_LOW_RANK, d), lambda j: (layer, lr0 // GLA_LOW_RANK, 0)),
        ],
        out_specs=[pl.BlockSpec((d, ct), lambda j: (0, j)), pl.BlockSpec((d, LANES), lambda j: (0, 0))],
        out_shape=[jax.ShapeDtypeStruct((d, n_main), BF16), jax.ShapeDtypeStruct((d, LANES), BF16)],
        compiler_params=_cparams(("arbitrary",)),
        name="regroup",
    )(w_in_t, w_in_t, w_in_t)


def _in_proj_kernel(x_ref, w_ref, wlr_ref, o_ref, lr_ref, xb_ref):
    @pl.when(pl.program_id(1) == 0)
    def _():
        xb = x_ref[...].astype(BF16)
        xb_ref[...] = xb
        lr_ref[...] = _dot(xb, wlr_ref[...]).astype(BF16)

    o_ref[...] = _dot(xb_ref[...], w_ref[...]).astype(BF16)


def _in_proj(x, w, wlr, tm, tn):
    m, d = x.shape
    n = w.shape[1]
    return pl.pallas_call(
        _in_proj_kernel,
        grid=(m // tm, n // tn),
        in_specs=[
            pl.BlockSpec((tm, d), lambda i, j: (i, 0)),
            pl.BlockSpec((d, tn), lambda i, j: (0, j)),
            pl.BlockSpec((d, LANES), lambda i, j: (0, 0)),
        ],
        out_specs=[
            pl.BlockSpec((tm, tn), lambda i, j: (i, j)),
            pl.BlockSpec((tm, LANES), lambda i, j: (i, 0)),
        ],
        out_shape=[jax.ShapeDtypeStruct((m, n), BF16), jax.ShapeDtypeStruct((m, LANES), BF16)],
        scratch_shapes=[pltpu.VMEM((tm, d), BF16)],
        compiler_params=_cparams(("parallel", "arbitrary")),
        name="in_proj",
    )(x, w, wlr)


def _gla_kernel(q_ref, k_ref, v_ref, g_ref, lr_ref, wa_ref, ba_ref, ng_ref, o_ref, s_ref, *, scale):
    tc, dk = q_ref.shape
    dv = v_ref.shape[1]
    c, blk = GLA_CHUNK, GLA_BLOCK
    nc = blk // c

    @pl.when(pl.program_id(2) == 0)
    def _():
        s_ref[...] = jnp.zeros_like(s_ref)

    alpha_pre = _dot(lr_ref[...], wa_ref[...]) + ba_ref[...]
    log_a = (jnp.minimum(alpha_pre, 0.0) - jnp.log1p(jnp.exp(-jnp.abs(alpha_pre)))) * (1.0 / GLA_TAU)

    row = lax.broadcasted_iota(jnp.int32, (blk, blk), 0)
    col = lax.broadcasted_iota(jnp.int32, (blk, blk), 1)
    lower = jnp.where(row >= col, 1.0, 0.0)
    tri = jnp.where((row // c) == (col // c), lower, 0.0).astype(BF16)
    zeros_chunk = jnp.zeros((c, dk), BF16)

    def rowsum(vs):
        out = None
        for v in vs:
            out = v if out is None else out + v
        return out

    def scaled(x, e):
        return x if e is None else x * jnp.exp(e)

    for bi in range(tc // blk):
        bs = slice(bi * blk, (bi + 1) * blk)
        la = log_a[bs]
        hi = la.astype(BF16)
        mid = (la - hi.astype(F32)).astype(BF16)
        r = _dot(tri, jnp.concatenate([hi, mid], axis=1))
        bc = r[:, :dk] + r[:, dk:]
        qf = q_ref[bs, :].astype(F32) * scale
        kf = k_ref[bs, :].astype(F32)
        vb = v_ref[bs, :]
        q_dec = qf * jnp.exp(bc)
        k_inv = (kf * jnp.exp(-bc)).astype(BF16)

        cs = [slice(j * c, (j + 1) * c) for j in range(nc)]
        bl = [bc[j * c + c - 1:j * c + c, :] for j in range(nc)]
        k_end = [kf[cs[j]] * jnp.exp(bl[j] - bc[cs[j]]) for j in range(nc)]

        strips = []
        for i in range(nc):
            rows = [scaled(k_end[j], rowsum(bl[j + 1:i])).astype(BF16) for j in range(i)]
            rows.append(k_inv[cs[i]])
            if (c * (i + 1)) % LANES:
                rows.append(zeros_chunk)
            keys_i = jnp.concatenate(rows, axis=0)
            w = keys_i.shape[0]
            sc = _dot_nt(q_dec[cs[i]].astype(BF16), keys_i)
            rr = lax.broadcasted_iota(jnp.int32, (c, w), 0)
            cc = lax.broadcasted_iota(jnp.int32, (c, w), 1)
            sc = jnp.where(cc <= rr + c * i, sc, 0.0).astype(BF16)
            if w < blk:
                sc = jnp.concatenate([sc, jnp.zeros((c, blk - w), BF16)], axis=1)
            strips.append(sc)
        p = jnp.concatenate(strips, axis=0)
        q_anch = jnp.concatenate(
            [scaled(q_dec[cs[i]], rowsum(bl[:i])).astype(BF16) for i in range(nc)], axis=0)
        k_anch = jnp.concatenate(
            [scaled(k_end[j], rowsum(bl[j + 1:])).astype(BF16) for j in range(nc)], axis=0)

        s = s_ref[...]
        o = _dot(jnp.concatenate([p, q_anch], axis=1),
                 jnp.concatenate([vb, s.astype(BF16)], axis=0))
        decay = jnp.exp(rowsum(bl))
        decay_t = jnp.transpose(jnp.broadcast_to(decay, (LANES, dk)))
        s_ref[...] = s * jnp.concatenate([decay_t] * (dv // LANES), axis=1) + _dot_tn(k_anch, vb)

        ms = jnp.mean(o * o, axis=-1, keepdims=True)
        y = o * lax.rsqrt(ms + GLA_NORM_EPS) * ng_ref[...]
        g = g_ref[bs, :].astype(F32)
        o_ref[bs, :] = (y * (g * jax.nn.sigmoid(g))).astype(BF16)


def _gla(proj, lr, wa, ba, ng, layer, batch, seq, tc, off):
    m = proj.shape[0]
    dk = wa.shape[2] // GLA_HEADS
    dv = ng.shape[2]
    nt = seq // tc
    rowmap = lambda b, h, t: b * nt + t
    return pl.pallas_call(
        functools.partial(_gla_kernel, scale=float(dk) ** -0.5),
        grid=(batch, GLA_HEADS, nt),
        in_specs=[
            pl.BlockSpec((tc, dk), lambda b, h, t: (rowmap(b, h, t), off["q"] // dk + h)),
            pl.BlockSpec((tc, dk), lambda b, h, t: (rowmap(b, h, t), off["k"] // dk + h)),
            pl.BlockSpec((tc, dv), lambda b, h, t: (rowmap(b, h, t), off["v"] // dv + h)),
            pl.BlockSpec((tc, dv), lambda b, h, t: (rowmap(b, h, t), off["g"] // dv + h)),
            pl.BlockSpec((tc, LANES), lambda b, h, t: (rowmap(b, h, t), 0)),
            pl.BlockSpec((None, LANES, dk), lambda b, h, t: (layer, 0, h)),
            pl.BlockSpec((None, 1, dk), lambda b, h, t: (layer, 0, h)),
            pl.BlockSpec((None, 1, dv), lambda b, h, t: (layer, 0, 0)),
        ],
        out_specs=pl.BlockSpec((tc, dv), lambda b, h, t: (rowmap(b, h, t), h)),
        out_shape=jax.ShapeDtypeStruct((m, GLA_HEADS * dv), BF16),
        scratch_shapes=[pltpu.VMEM((dk, dv), F32)],
        compiler_params=_cparams(("parallel", "parallel", "arbitrary")),
        name="gla",
    )(proj, proj, proj, proj, lr, wa, ba, ng)


def _lru_kernel(rx_ref, ry_ref, cw_ref, cb_ref, wa_ref, ba_ref, wx_ref, bx_ref, lam_ref, o_ref,
                xbuf_ref, a_ref, u_ref, hc_ref):
    tc, wc = rx_ref.shape
    gb = wc // LRU_BLOCK
    t = pl.program_id(2)

    @pl.when(t == 0)
    def _():
        xbuf_ref[0:SUBLANES, :] = jnp.zeros((SUBLANES, wc), F32)
        hc_ref[...] = jnp.zeros_like(hc_ref)

    @pl.when(t > 0)
    def _():
        xbuf_ref[0:SUBLANES, :] = xbuf_ref[tc:tc + SUBLANES, :]

    xbuf_ref[SUBLANES:SUBLANES + tc, :] = rx_ref[...].astype(F32)

    hx = cb_ref[...] + cw_ref[CONV_WIDTH - 1:CONV_WIDTH, :] * xbuf_ref[SUBLANES:SUBLANES + tc, :]
    for j in range(CONV_WIDTH - 1):
        start = SUBLANES - (CONV_WIDTH - 1) + j
        hx = hx + cw_ref[j:j + 1, :] * xbuf_ref[start:start + tc, :]

    lam = lam_ref[...]
    neg_c_softplus = -LRU_C * (jnp.maximum(-lam, 0.0) + jnp.log1p(jnp.exp(-jnp.abs(lam))))
    for gi in range(gb):
        cs = slice(gi * LRU_BLOCK, (gi + 1) * LRU_BLOCK)
        hxg = hx[:, cs]
        hxb = hxg.astype(BF16)
        r = jax.nn.sigmoid(_dot(hxb, wa_ref[gi]) + ba_ref[:, cs])
        i = jax.nn.sigmoid(_dot(hxb, wx_ref[gi]) + bx_ref[:, cs])
        a = jnp.exp(neg_c_softplus[:, cs] * r)
        y = 1.0 - a * a
        a_ref[:, cs] = a
        u_ref[:, cs] = (y * lax.rsqrt(jnp.maximum(y, F32_TINY))) * (i * hxg)

    row = lax.broadcasted_iota(jnp.int32, (SUBLANES, wc), 0)

    def body(s, carry):
        r0 = pl.multiple_of(s * SUBLANES, SUBLANES)
        a8 = a_ref[pl.ds(r0, SUBLANES), :]
        u8 = u_ref[pl.ds(r0, SUBLANES), :]
        for d in (1, 2, 4):
            keep = row >= d
            a_sh = pltpu.roll(a8, d, 0)
            u_sh = pltpu.roll(u8, d, 0)
            u8 = jnp.where(keep, a8 * u_sh + u8, u8)
            a8 = jnp.where(keep, a8 * a_sh, a8)
        h8 = u8 + a8 * carry
        u_ref[pl.ds(r0, SUBLANES), :] = h8
        return h8[SUBLANES - 1:SUBLANES, :]

    hc_ref[...] = lax.fori_loop(0, tc // SUBLANES, body, hc_ref[...], unroll=4)

    ry = ry_ref[...].astype(F32)
    th = jnp.tanh(ry * (GELU_C0 + GELU_C1 * (ry * ry)))
    o_ref[...] = (u_ref[...] * (0.5 * ry) * (1.0 + th)).astype(BF16)


def _lru(proj, cw, cb, wa, ba, wx, bx, lam, layer, batch, seq, tc, wc, off):
    m = proj.shape[0]
    width = cw.shape[2]
    gb = wc // LRU_BLOCK
    nt = seq // tc
    rowmap = lambda b, n, t: b * nt + t
    vec = lambda: pl.BlockSpec((None, 1, wc), lambda b, n, t: (layer, 0, n))
    gate_w = lambda: pl.BlockSpec((None, gb, LRU_BLOCK, LRU_BLOCK), lambda b, n, t: (layer, n, 0, 0))
    return pl.pallas_call(
        _lru_kernel,
        grid=(batch, width // wc, nt),
        in_specs=[
            pl.BlockSpec((tc, wc), lambda b, n, t: (rowmap(b, n, t), off["rx"] // wc + n)),
            pl.BlockSpec((tc, wc), lambda b, n, t: (rowmap(b, n, t), off["ry"] // wc + n)),
            pl.BlockSpec((None, CONV_WIDTH, wc), lambda b, n, t: (layer, 0, n)),
            vec(), gate_w(), vec(), gate_w(), vec(), vec(),
        ],
        out_specs=pl.BlockSpec((tc, wc), lambda b, n, t: (rowmap(b, n, t), n)),
        out_shape=jax.ShapeDtypeStruct((m, width), BF16),
        scratch_shapes=[
            pltpu.VMEM((tc + SUBLANES, wc), F32),
            pltpu.VMEM((tc, wc), F32),
            pltpu.VMEM((tc, wc), F32),
            pltpu.VMEM((1, wc), F32),
        ],
        compiler_params=_cparams(("parallel", "parallel", "arbitrary")),
        name="lru",
    )(proj, proj, cw, cb, wa, ba, wx, bx, lam)


def _post_kernel(za_ref, zb_ref, ga_ref, gb_ref, h_ref, woa_ref, wob_ref, wout_ref, lg_ref, lb_ref,
                 o_ref, ob_ref, wa_s, wb_s, wo_s, *, alpha, npre):
    s = pl.program_id(0)
    rc = woa_ref.shape[0]

    @pl.when(s < npre)
    def _():
        r0 = pl.multiple_of(s * rc, rc)
        wa_s[pl.ds(r0, rc), :] = woa_ref[...].astype(BF16)
        wb_s[pl.ds(r0, rc), :] = wob_ref[...].astype(BF16)
        wo_s[pl.ds(r0, rc), :] = wout_ref[...].astype(BF16)

    @pl.when(s >= npre)
    def _():
        tm = za_ref.shape[0]
        mc = min(tm, RESIDENT_ROW_CHUNK)
        for mi in range(tm // mc):
            rs = slice(mi * mc, (mi + 1) * mc)
            ya = _dot(za_ref[rs, :], wa_s[...])
            yb = _dot(zb_ref[rs, :], wb_s[...])
            merged = (jax.nn.sigmoid(ga_ref[rs, :].astype(F32)) * ya
                      + jax.nn.sigmoid(gb_ref[rs, :].astype(F32)) * yb)
            mix = _dot(merged.astype(BF16), wo_s[...])
            h1 = _layer_norm(alpha * h_ref[rs, :] + mix, lg_ref[...], lb_ref[...])
            o_ref[rs, :] = h1
            ob_ref[rs, :] = h1.astype(BF16)


def _post(za, zb, proj, h, woa, wob, wout, lg, lb, layer, tm, rc, off, alpha):
    m, d = h.shape
    npre = d // rc
    tile = lambda s: jnp.maximum(s - npre, 0)
    chunk = lambda s: jnp.minimum(s, npre - 1)
    act = lambda col: pl.BlockSpec((tm, d), lambda s: (tile(s), col))
    wchunk = lambda: pl.BlockSpec((None, rc, d), lambda s: (layer, chunk(s), 0))
    vec = lambda: pl.BlockSpec((None, 1, d), lambda s: (layer, 0, 0))
    return pl.pallas_call(
        functools.partial(_post_kernel, alpha=alpha, npre=npre),
        grid=(npre + m // tm,),
        in_specs=[act(0), act(0), act(off["ga"] // d), act(off["gb"] // d), act(0),
                  wchunk(), wchunk(), wchunk(), vec(), vec()],
        out_specs=[act(0), act(0)],
        out_shape=[jax.ShapeDtypeStruct((m, d), F32), jax.ShapeDtypeStruct((m, d), BF16)],
        scratch_shapes=[pltpu.VMEM((d, d), BF16)] * 3,
        compiler_params=_cparams(("arbitrary",)),
        name="post",
    )(za, zb, proj, proj, h, woa, wob, wout, lg, lb)


def _gate_up_kernel(x_ref, wg_ref, wu_ref, o_ref, w_s):
    tf = wg_ref.shape[1]

    @pl.when(pl.program_id(1) == 0)
    def _():
        w_s[:, :tf] = wg_ref[...].astype(BF16)
        w_s[:, tf:] = wu_ref[...].astype(BF16)

    tm = x_ref.shape[0]
    mc = min(tm, GATE_UP_ROW_CHUNK)
    for mi in range(tm // mc):
        rs = slice(mi * mc, (mi + 1) * mc)
        r = _dot(x_ref[rs, :], w_s[...])
        gate, up = r[:, :tf], r[:, tf:]
        o_ref[rs, :] = (gate * jax.nn.sigmoid(gate) * up).astype(BF16)


def _gate_up(xb, wgu, layer, tm, tf):
    m, d = xb.shape
    dff = wgu.shape[2] // 2
    nf = dff // tf
    return pl.pallas_call(
        _gate_up_kernel,
        grid=(nf, m // tm),
        in_specs=[
            pl.BlockSpec((tm, d), lambda j, i: (i, 0)),
            pl.BlockSpec((None, d, tf), lambda j, i: (layer, 0, j)),
            pl.BlockSpec((None, d, tf), lambda j, i: (layer, 0, nf + j)),
        ],
        out_specs=pl.BlockSpec((tm, tf), lambda j, i: (i, j)),
        out_shape=jax.ShapeDtypeStruct((m, dff), BF16),
        scratch_shapes=[pltpu.VMEM((d, 2 * tf), BF16)],
        compiler_params=_cparams(("arbitrary", "arbitrary")),
        name="gate_up",
    )(xb, wgu, wgu)


def _down_kernel(a_ref, h_ref, wd_ref, lg_ref, lb_ref, o_ref, ob_ref, wd_s, *, alpha, npre):
    s = pl.program_id(0)
    rc = wd_ref.shape[0]

    @pl.when(s < npre)
    def _():
        r0 = pl.multiple_of(s * rc, rc)
        wd_s[pl.ds(r0, rc), :] = wd_ref[...].astype(BF16)

    @pl.when(s >= npre)
    def _():
        tm = a_ref.shape[0]
        mc = min(tm, RESIDENT_ROW_CHUNK)
        for mi in range(tm // mc):
            rs = slice(mi * mc, (mi + 1) * mc)
            y = _dot(a_ref[rs, :], wd_s[...])
            h2 = _layer_norm(alpha * h_ref[rs, :] + y, lg_ref[...], lb_ref[...])
            o_ref[rs, :] = h2
            ob_ref[rs, :] = h2.astype(BF16)


def _down(act, h, wd, lg, lb, layer, tm, rc, alpha):
    m, d = h.shape
    dff = wd.shape[1]
    npre = dff // rc
    tile = lambda s: jnp.maximum(s - npre, 0)
    chunk = lambda s: jnp.minimum(s, npre - 1)
    return pl.pallas_call(
        functools.partial(_down_kernel, alpha=alpha, npre=npre),
        grid=(npre + m // tm,),
        in_specs=[
            pl.BlockSpec((tm, dff), lambda s: (tile(s), 0)),
            pl.BlockSpec((tm, d), lambda s: (tile(s), 0)),
            pl.BlockSpec((None, rc, d), lambda s: (layer, chunk(s), 0)),
            pl.BlockSpec((None, 1, d), lambda s: (layer, 0, 0)),
            pl.BlockSpec((None, 1, d), lambda s: (layer, 0, 0)),
        ],
        out_specs=[pl.BlockSpec((tm, d), lambda s: (tile(s), 0)),
                   pl.BlockSpec((tm, d), lambda s: (tile(s), 0))],
        out_shape=[jax.ShapeDtypeStruct((m, d), F32), jax.ShapeDtypeStruct((m, d), BF16)],
        scratch_shapes=[pltpu.VMEM((dff, d), BF16)],
        compiler_params=_cparams(("arbitrary",)),
        name="down",
    )(act, h, wd, lg, lb)


def _largest_tile(n, cap, quantum):
    t = min(cap, n) // quantum * quantum
    while n % t:
        t -= quantum
    return t


def kernel(x, w_in, gla_w_alpha, gla_b_alpha, gla_norm_g, gla_w_o, conv_w, conv_b, lru_w_a, lru_b_a, lru_w_x,
           lru_b_x, lru_lambda, rnn_w_o, w_out, ln1_g, ln1_b, w_gate_up, w_down, ln2_g, ln2_b):
    batch, seq, d = x.shape
    depth = w_in.shape[0]
    m = batch * seq
    key_w = gla_w_alpha.shape[2]
    val_w = gla_w_o.shape[1]
    lru_w = conv_w.shape[2]
    dff = w_down.shape[1]
    alpha = (2.0 * depth) ** 0.25

    lr0 = 2 * key_w + 2 * val_w
    off = {"q": 0, "k": key_w, "v": 2 * key_w, "g": 2 * key_w + val_w,
           "rx": lr0, "ry": lr0 + lru_w, "ga": lr0 + 2 * lru_w, "gb": lr0 + 2 * lru_w + d}
    n_main = lr0 + 2 * lru_w + 2 * d

    tm_proj = _largest_tile(seq, 1024, 256)
    tn_proj = _largest_tile(n_main, 2048, 512)
    tc_gla = _largest_tile(seq, 512, GLA_BLOCK)
    tc_lru = _largest_tile(seq, 512, SUBLANES)
    tm_post = _largest_tile(seq, 256, 128)
    rc_post = 128
    tm_gu = _largest_tile(m, 2048, 256)
    tf_gu = _largest_tile(dff, 512, 256)
    tm_down = _largest_tile(seq, 256, 128)
    rc_down = 256
    ct_regroup = _largest_tile(lr0, 1024, LANES)
    w_in_t = jnp.swapaxes(w_in, 1, 2)

    row3 = lambda p: p[:, None, :]
    w_alpha = jnp.pad(gla_w_alpha, ((0, 0), (0, LANES - GLA_LOW_RANK), (0, 0))).astype(BF16)
    b_alpha, norm_g = row3(gla_b_alpha), row3(gla_norm_g)
    cb, ba, bx, lam = row3(conv_b), row3(lru_b_a), row3(lru_b_x), row3(lru_lambda)
    wa_b, wx_b = lru_w_a.astype(BF16), lru_w_x.astype(BF16)
    l1g, l1b, l2g, l2b = row3(ln1_g), row3(ln1_b), row3(ln2_g), row3(ln2_b)

    h = x.reshape(m, d)
    hb = h
    for l in range(depth):
        w_main, w_lr = _regroup(w_in_t, l, lr0, ct_regroup)
        proj, lr = _in_proj(hb, w_main, w_lr, tm_proj, tn_proj)
        za = _gla(proj, lr, w_alpha, b_alpha, norm_g, l, batch, seq, tc_gla, off)
        zb = _lru(proj, conv_w, cb, wa_b, ba, wx_b, bx, lam, l, batch, seq, tc_lru, 4 * LRU_BLOCK, off)
        h, hb = _post(za, zb, proj, h, gla_w_o, rnn_w_o, w_out, l1g, l1b, l, tm_post, rc_post, off, alpha)
        act = _gate_up(hb, w_gate_up, l, tm_gu, tf_gu)
        h, hb = _down(act, h, w_down, l2g, l2b, l, tm_down, rc_down, alpha)
    return h.reshape(batch, seq, d)
```

```python
import functools

import jax
import jax.numpy as jnp
from jax import lax
from jax.experimental import pallas as pl
from jax.experimental.pallas import tpu as pltpu

F32 = jnp.float32
BF16 = jnp.bfloat16

GLA_HEADS = 4
GLA_LOW_RANK = 16
GLA_TAU = 16.0
GLA_CHUNK = 64
GLA_BLOCK = 256
GLA_NORM_EPS = 1e-5
LRU_BLOCK = 256
LRU_C = 8.0
CONV_WIDTH = 4
LN_EPS = 1e-5
GELU_C0 = 0.7978845608028654
GELU_C1 = 0.044715 * GELU_C0
GATE_UP_ROW_CHUNK = 512
RESIDENT_ROW_CHUNK = 128

LANES = 128
SUBLANES = 8
VMEM_LIMIT_BYTES = 56 * 1024 * 1024
F32_TINY = 1e-30


def _cparams(sem):
    return pltpu.CompilerParams(dimension_semantics=sem, vmem_limit_bytes=VMEM_LIMIT_BYTES)


def _dot(a, b):
    return jnp.dot(a, b, preferred_element_type=F32)


def _dot_nt(a, b):
    return lax.dot_general(a, b, (((1,), (1,)), ((), ())), preferred_element_type=F32)


def _dot_tn(a, b):
    return lax.dot_general(a, b, (((0,), (0,)), ((), ())), preferred_element_type=F32)


def _layer_norm(z, g, b):
    mu = jnp.mean(z, axis=-1, keepdims=True)
    zc = z - mu
    var = jnp.mean(zc * zc, axis=-1, keepdims=True)
    return zc * lax.rsqrt(var + LN_EPS) * g + b


def _regroup_kernel(a_ref, b_ref, c_ref, o_ref, lr_ref, *, gap_tile):
    j = pl.program_id(0)
    d = a_ref.shape[1]

    @pl.when(j < gap_tile)
    def _():
        o_ref[...] = a_ref[...].T.astype(BF16)

    @pl.when(j >= gap_tile)
    def _():
        src = jnp.concatenate([a_ref[GLA_LOW_RANK:, :], b_ref[...]], axis=0)
        o_ref[...] = src.T.astype(BF16)

    @pl.when(j == 0)
    def _():
        rows = jnp.concatenate([c_ref[...], jnp.zeros((LANES - GLA_LOW_RANK, d), F32)], axis=0)
        lr_ref[...] = rows.T.astype(BF16)


def _regroup(w_in_t, layer, lr0, ct):
    _, n_in, d = w_in_t.shape
    n_main = n_in - GLA_LOW_RANK
    assert lr0 % ct == 0 and n_main % ct == 0 and ct % GLA_LOW_RANK == 0
    sub = ct // GLA_LOW_RANK
    return pl.pallas_call(
        functools.partial(_regroup_kernel, gap_tile=lr0 // ct),
        grid=(n_main // ct,),
        in_specs=[
            pl.BlockSpec((None, ct, d), lambda j: (layer, j, 0)),
            pl.BlockSpec((None, GLA_LOW_RANK, d), lambda j: (layer, (j + 1) * sub, 0)),
            pl.BlockSpec((None, GLA_LOW_RANK, d), lambda j: (layer, lr0 // GLA_LOW_RANK, 0)),
        ],
        out_specs=[pl.BlockSpec((d, ct), lambda j: (0, j)), pl.BlockSpec((d, LANES), lambda j: (0, 0))],
        out_shape=[jax.ShapeDtypeStruct((d, n_main), BF16), jax.ShapeDtypeStruct((d, LANES), BF16)],
        compiler_params=_cparams(("arbitrary",)),
        name="regroup",
    )(w_in_t, w_in_t, w_in_t)


def _in_proj_kernel(x_ref, w_ref, wlr_ref, c0_ref, c1_ref, c2_ref, o_ref, lr_ref, b0_ref, b1_ref, b2_ref,
                    xb_ref, *, n_cast):
    j = pl.program_id(1)

    @pl.when(j == 0)
    def _():
        xb = x_ref[...].astype(BF16)
        xb_ref[...] = xb
        lr_ref[...] = _dot(xb, wlr_ref[...]).astype(BF16)

    @pl.when(pl.program_id(0) * pl.num_programs(1) + j < n_cast)
    def _():
        b0_ref[...] = c0_ref[...].astype(BF16)
        b1_ref[...] = c1_ref[...].astype(BF16)
        b2_ref[...] = c2_ref[...].astype(BF16)

    o_ref[...] = _dot(xb_ref[...], w_ref[...]).astype(BF16)


def _in_proj(x, w, wlr, cast_ws, layer, tm, tn):
    m, d = x.shape
    n = w.shape[1]
    nj = n // tn
    rows = cast_ws[0].shape[1]
    rc = _cast_rows(rows, (m // tm) * nj)
    n_cast = rows // rc
    chunk = lambda i, j: jnp.minimum(i * nj + j, n_cast - 1)
    cast_in = lambda cw: pl.BlockSpec((None, rc, cw.shape[2]), lambda i, j: (layer, chunk(i, j), 0))
    cast_out = lambda cw: pl.BlockSpec((rc, cw.shape[2]), lambda i, j: (chunk(i, j), 0))
    return pl.pallas_call(
        functools.partial(_in_proj_kernel, n_cast=n_cast),
        grid=(m // tm, nj),
        in_specs=[
            pl.BlockSpec((tm, d), lambda i, j: (i, 0)),
            pl.BlockSpec((d, tn), lambda i, j: (0, j)),
            pl.BlockSpec((d, LANES), lambda i, j: (0, 0)),
        ] + [cast_in(cw) for cw in cast_ws],
        out_specs=[
            pl.BlockSpec((tm, tn), lambda i, j: (i, j)),
            pl.BlockSpec((tm, LANES), lambda i, j: (i, 0)),
        ] + [cast_out(cw) for cw in cast_ws],
        out_shape=[jax.ShapeDtypeStruct((m, n), BF16), jax.ShapeDtypeStruct((m, LANES), BF16)]
        + [jax.ShapeDtypeStruct(cw.shape[1:], BF16) for cw in cast_ws],
        scratch_shapes=[pltpu.VMEM((tm, d), BF16)],
        compiler_params=_cparams(("arbitrary", "arbitrary")),
        name="in_proj",
    )(x, w, wlr, *cast_ws)


def _gla_kernel(q_ref, k_ref, v_ref, g_ref, lr_ref, wa_ref, ba_ref, ng_ref, o_ref, s_ref, *, scale):
    tc, dk = q_ref.shape
    dv = v_ref.shape[1]
    c, blk = GLA_CHUNK, GLA_BLOCK
    nc = blk // c

    @pl.when(pl.program_id(2) == 0)
    def _():
        s_ref[...] = jnp.zeros_like(s_ref)

    alpha_pre = _dot(lr_ref[...], wa_ref[...]) + ba_ref[...]
    log_a = (jnp.minimum(alpha_pre, 0.0) - jnp.log1p(jnp.exp(-jnp.abs(alpha_pre)))) * (1.0 / GLA_TAU)

    row = lax.broadcasted_iota(jnp.int32, (blk, blk), 0)
    col = lax.broadcasted_iota(jnp.int32, (blk, blk), 1)
    lower = jnp.where(row >= col, 1.0, 0.0)
    tri = jnp.where((row // c) == (col // c), lower, 0.0).astype(BF16)
    zeros_chunk = jnp.zeros((c, dk), BF16)

    def rowsum(vs):
        out = None
        for v in vs:
            out = v if out is None else out + v
        return out

    def scaled(x, e):
        return x if e is None else x * jnp.exp(e)

    for bi in range(tc // blk):
        bs = slice(bi * blk, (bi + 1) * blk)
        la = log_a[bs]
        hi = la.astype(BF16)
        mid = (la - hi.astype(F32)).astype(BF16)
        r = _dot(tri, jnp.concatenate([hi, mid], axis=1))
        bc = r[:, :dk] + r[:, dk:]
        qf = q_ref[bs, :].astype(F32) * scale
        kf = k_ref[bs, :].astype(F32)
        vb = v_ref[bs, :]
        q_dec = qf * jnp.exp(bc)
        k_inv = (kf * jnp.exp(-bc)).astype(BF16)

        cs = [slice(j * c, (j + 1) * c) for j in range(nc)]
        bl = [bc[j * c + c - 1:j * c + c, :] for j in range(nc)]
        k_end = [kf[cs[j]] * jnp.exp(bl[j] - bc[cs[j]]) for j in range(nc)]

        strips = []
        for i in range(nc):
            rows = [scaled(k_end[j], rowsum(bl[j + 1:i])).astype(BF16) for j in range(i)]
            rows.append(k_inv[cs[i]])
            if (c * (i + 1)) % LANES:
                rows.append(zeros_chunk)
            keys_i = jnp.concatenate(rows, axis=0)
            w = keys_i.shape[0]
            sc = _dot_nt(q_dec[cs[i]].astype(BF16), keys_i)
            rr = lax.broadcasted_iota(jnp.int32, (c, w), 0)
            cc = lax.broadcasted_iota(jnp.int32, (c, w), 1)
            sc = jnp.where(cc <= rr + c * i, sc, 0.0).astype(BF16)
            if w < blk:
                sc = jnp.concatenate([sc, jnp.zeros((c, blk - w), BF16)], axis=1)
            strips.append(sc)
        p = jnp.concatenate(strips, axis=0)
        q_anch = jnp.concatenate(
            [scaled(q_dec[cs[i]], rowsum(bl[:i])).astype(BF16) for i in range(nc)], axis=0)
        k_anch = jnp.concatenate(
            [scaled(k_end[j], rowsum(bl[j + 1:])).astype(BF16) for j in range(nc)], axis=0)

        s = s_ref[...]
        o = _dot(jnp.concatenate([p, q_anch], axis=1),
                 jnp.concatenate([vb, s.astype(BF16)], axis=0))
        decay = jnp.exp(rowsum(bl))
        decay_t = jnp.transpose(jnp.broadcast_to(decay, (LANES, dk)))
        s_ref[...] = s * jnp.concatenate([decay_t] * (dv // LANES), axis=1) + _dot_tn(k_anch, vb)

        ms = jnp.mean(o * o, axis=-1, keepdims=True)
        y = o * lax.rsqrt(ms + GLA_NORM_EPS) * ng_ref[...]
        g = g_ref[bs, :].astype(F32)
        o_ref[bs, :] = (y * (g * jax.nn.sigmoid(g))).astype(BF16)


def _gla(proj, lr, wa, ba, ng, layer, batch, seq, tc, off):
    m = proj.shape[0]
    dk = wa.shape[2] // GLA_HEADS
    dv = ng.shape[2]
    nt = seq // tc
    rowmap = lambda b, h, t: b * nt + t
    return pl.pallas_call(
        functools.partial(_gla_kernel, scale=float(dk) ** -0.5),
        grid=(batch, GLA_HEADS, nt),
        in_specs=[
            pl.BlockSpec((tc, dk), lambda b, h, t: (rowmap(b, h, t), off["q"] // dk + h)),
            pl.BlockSpec((tc, dk), lambda b, h, t: (rowmap(b, h, t), off["k"] // dk + h)),
            pl.BlockSpec((tc, dv), lambda b, h, t: (rowmap(b, h, t), off["v"] // dv + h)),
            pl.BlockSpec((tc, dv), lambda b, h, t: (rowmap(b, h, t), off["g"] // dv + h)),
            pl.BlockSpec((tc, LANES), lambda b, h, t: (rowmap(b, h, t), 0)),
            pl.BlockSpec((None, LANES, dk), lambda b, h, t: (layer, 0, h)),
            pl.BlockSpec((None, 1, dk), lambda b, h, t: (layer, 0, h)),
            pl.BlockSpec((None, 1, dv), lambda b, h, t: (layer, 0, 0)),
        ],
        out_specs=pl.BlockSpec((tc, dv), lambda b, h, t: (rowmap(b, h, t), h)),
        out_shape=jax.ShapeDtypeStruct((m, GLA_HEADS * dv), BF16),
        scratch_shapes=[pltpu.VMEM((dk, dv), F32)],
        compiler_params=_cparams(("parallel", "parallel", "arbitrary")),
        name="gla",
    )(proj, proj, proj, proj, lr, wa, ba, ng)


def _lru_kernel(rx_ref, ry_ref, cw_ref, cb_ref, wa_ref, ba_ref, wx_ref, bx_ref, lam_ref, o_ref,
                xbuf_ref, a_ref, u_ref, hc_ref):
    tc, wc = rx_ref.shape
    gb = wc // LRU_BLOCK
    t = pl.program_id(2)

    @pl.when(t == 0)
    def _():
        xbuf_ref[0:SUBLANES, :] = jnp.zeros((SUBLANES, wc), F32)
        hc_ref[...] = jnp.zeros_like(hc_ref)

    @pl.when(t > 0)
    def _():
        xbuf_ref[0:SUBLANES, :] = xbuf_ref[tc:tc + SUBLANES, :]

    xbuf_ref[SUBLANES:SUBLANES + tc, :] = rx_ref[...].astype(F32)

    hx = cb_ref[...] + cw_ref[CONV_WIDTH - 1:CONV_WIDTH, :] * xbuf_ref[SUBLANES:SUBLANES + tc, :]
    for j in range(CONV_WIDTH - 1):
        start = SUBLANES - (CONV_WIDTH - 1) + j
        hx = hx + cw_ref[j:j + 1, :] * xbuf_ref[start:start + tc, :]

    lam = lam_ref[...]
    neg_c_softplus = -LRU_C * (jnp.maximum(-lam, 0.0) + jnp.log1p(jnp.exp(-jnp.abs(lam))))
    for gi in range(gb):
        cs = slice(gi * LRU_BLOCK, (gi + 1) * LRU_BLOCK)
        hxg = hx[:, cs]
        hxb = hxg.astype(BF16)
        r = jax.nn.sigmoid(_dot(hxb, wa_ref[gi]) + ba_ref[:, cs])
        i = jax.nn.sigmoid(_dot(hxb, wx_ref[gi]) + bx_ref[:, cs])
        a = jnp.exp(neg_c_softplus[:, cs] * r)
        y = 1.0 - a * a
        a_ref[:, cs] = a
        u_ref[:, cs] = (y * lax.rsqrt(jnp.maximum(y, F32_TINY))) * (i * hxg)

    row = lax.broadcasted_iota(jnp.int32, (SUBLANES, wc), 0)

    def body(s, carry):
        r0 = pl.multiple_of(s * SUBLANES, SUBLANES)
        a8 = a_ref[pl.ds(r0, SUBLANES), :]
        u8 = u_ref[pl.ds(r0, SUBLANES), :]
        for d in (1, 2, 4):
            keep = row >= d
            a_sh = pltpu.roll(a8, d, 0)
            u_sh = pltpu.roll(u8, d, 0)
            u8 = jnp.where(keep, a8 * u_sh + u8, u8)
            a8 = jnp.where(keep, a8 * a_sh, a8)
        h8 = u8 + a8 * carry
        u_ref[pl.ds(r0, SUBLANES), :] = h8
        return h8[SUBLANES - 1:SUBLANES, :]

    hc_ref[...] = lax.fori_loop(0, tc // SUBLANES, body, hc_ref[...], unroll=4)

    ry = ry_ref[...].astype(F32)
    th = jnp.tanh(ry * (GELU_C0 + GELU_C1 * (ry * ry)))
    o_ref[...] = (u_ref[...] * (0.5 * ry) * (1.0 + th)).astype(BF16)


def _lru(proj, cw, cb, wa, ba, wx, bx, lam, layer, batch, seq, tc, wc, off):
    m = proj.shape[0]
    width = cw.shape[2]
    gb = wc // LRU_BLOCK
    nt = seq // tc
    rowmap = lambda b, n, t: b * nt + t
    vec = lambda: pl.BlockSpec((None, 1, wc), lambda b, n, t: (layer, 0, n))
    gate_w = lambda: pl.BlockSpec((None, gb, LRU_BLOCK, LRU_BLOCK), lambda b, n, t: (layer, n, 0, 0))
    return pl.pallas_call(
        _lru_kernel,
        grid=(batch, width // wc, nt),
        in_specs=[
            pl.BlockSpec((tc, wc), lambda b, n, t: (rowmap(b, n, t), off["rx"] // wc + n)),
            pl.BlockSpec((tc, wc), lambda b, n, t: (rowmap(b, n, t), off["ry"] // wc + n)),
            pl.BlockSpec((None, CONV_WIDTH, wc), lambda b, n, t: (layer, 0, n)),
            vec(), gate_w(), vec(), gate_w(), vec(), vec(),
        ],
        out_specs=pl.BlockSpec((tc, wc), lambda b, n, t: (rowmap(b, n, t), n)),
        out_shape=jax.ShapeDtypeStruct((m, width), BF16),
        scratch_shapes=[
            pltpu.VMEM((tc + SUBLANES, wc), F32),
            pltpu.VMEM((tc, wc), F32),
            pltpu.VMEM((tc, wc), F32),
            pltpu.VMEM((1, wc), F32),
        ],
        compiler_params=_cparams(("parallel", "parallel", "arbitrary")),
        name="lru",
    )(proj, proj, cw, cb, wa, ba, wx, bx, lam)


def _post_kernel(za_ref, zb_ref, ga_ref, gb_ref, h_ref, wa_ref, wb_ref, wo_ref, lg_ref, lb_ref,
                 o_ref, ob_ref, *, alpha):
    tm = za_ref.shape[0]
    mc = min(tm, RESIDENT_ROW_CHUNK)
    for mi in range(tm // mc):
        rs = slice(mi * mc, (mi + 1) * mc)
        ya = _dot(za_ref[rs, :], wa_ref[...])
        yb = _dot(zb_ref[rs, :], wb_ref[...])
        merged = (jax.nn.sigmoid(ga_ref[rs, :].astype(F32)) * ya
                  + jax.nn.sigmoid(gb_ref[rs, :].astype(F32)) * yb)
        mix = _dot(merged.astype(BF16), wo_ref[...])
        h1 = _layer_norm(alpha * h_ref[rs, :] + mix, lg_ref[...], lb_ref[...])
        o_ref[rs, :] = h1
        ob_ref[rs, :] = h1.astype(BF16)


def _post(za, zb, proj, h, woa, wob, wout, lg, lb, layer, tm, off, alpha):
    m, d = h.shape
    act = lambda col: pl.BlockSpec((tm, d), lambda i: (i, col))
    resident = lambda w: pl.BlockSpec(w.shape, lambda i: (0, 0), pipeline_mode=pl.Buffered(1))
    vec = lambda: pl.BlockSpec((None, 1, d), lambda i: (layer, 0, 0))
    return pl.pallas_call(
        functools.partial(_post_kernel, alpha=alpha),
        grid=(m // tm,),
        in_specs=[act(0), act(0), act(off["ga"] // d), act(off["gb"] // d), act(0),
                  resident(woa), resident(wob), resident(wout), vec(), vec()],
        out_specs=[act(0), act(0)],
        out_shape=[jax.ShapeDtypeStruct((m, d), F32), jax.ShapeDtypeStruct((m, d), BF16)],
        compiler_params=_cparams(("parallel",)),
        name="post",
    )(za, zb, proj, proj, h, woa, wob, wout, lg, lb)


def _gate_up_kernel(x_ref, wg_ref, wu_ref, c_ref, o_ref, b_ref, w_s, *, n_cast):
    tf = wg_ref.shape[1]
    i = pl.program_id(1)

    @pl.when(i == 0)
    def _():
        w_s[:, :tf] = wg_ref[...].astype(BF16)
        w_s[:, tf:] = wu_ref[...].astype(BF16)

    @pl.when(pl.program_id(0) * pl.num_programs(1) + i < n_cast)
    def _():
        b_ref[...] = c_ref[...].astype(BF16)

    tm = x_ref.shape[0]
    mc = min(tm, GATE_UP_ROW_CHUNK)
    for mi in range(tm // mc):
        rs = slice(mi * mc, (mi + 1) * mc)
        r = _dot(x_ref[rs, :], w_s[...])
        gate, up = r[:, :tf], r[:, tf:]
        o_ref[rs, :] = (gate * jax.nn.sigmoid(gate) * up).astype(BF16)


def _gate_up(xb, wgu, wd, layer, tm, tf):
    m, d = xb.shape
    dff = wgu.shape[2] // 2
    nf = dff // tf
    ni = m // tm
    rc = _cast_rows(dff, nf * ni)
    n_cast = dff // rc
    chunk = lambda j, i: jnp.minimum(j * ni + i, n_cast - 1)
    return pl.pallas_call(
        functools.partial(_gate_up_kernel, n_cast=n_cast),
        grid=(nf, ni),
        in_specs=[
            pl.BlockSpec((tm, d), lambda j, i: (i, 0)),
            pl.BlockSpec((None, d, tf), lambda j, i: (layer, 0, j)),
            pl.BlockSpec((None, d, tf), lambda j, i: (layer, 0, nf + j)),
            pl.BlockSpec((None, rc, d), lambda j, i: (layer, chunk(j, i), 0)),
        ],
        out_specs=[pl.BlockSpec((tm, tf), lambda j, i: (i, j)),
                   pl.BlockSpec((rc, d), lambda j, i: (chunk(j, i), 0))],
        out_shape=[jax.ShapeDtypeStruct((m, dff), BF16), jax.ShapeDtypeStruct((dff, d), BF16)],
        scratch_shapes=[pltpu.VMEM((d, 2 * tf), BF16)],
        compiler_params=_cparams(("arbitrary", "arbitrary")),
        name="gate_up",
    )(xb, wgu, wgu, wd)


def _down_kernel(a_ref, h_ref, wd_ref, lg_ref, lb_ref, o_ref, ob_ref, *, alpha):
    tm = a_ref.shape[0]
    mc = min(tm, RESIDENT_ROW_CHUNK)
    for mi in range(tm // mc):
        rs = slice(mi * mc, (mi + 1) * mc)
        y = _dot(a_ref[rs, :], wd_ref[...])
        h2 = _layer_norm(alpha * h_ref[rs, :] + y, lg_ref[...], lb_ref[...])
        o_ref[rs, :] = h2
        ob_ref[rs, :] = h2.astype(BF16)


def _down(act, h, wd, lg, lb, layer, tm, alpha):
    m, d = h.shape
    dff = wd.shape[0]
    return pl.pallas_call(
        functools.partial(_down_kernel, alpha=alpha),
        grid=(m // tm,),
        in_specs=[
            pl.BlockSpec((tm, dff), lambda i: (i, 0)),
            pl.BlockSpec((tm, d), lambda i: (i, 0)),
            pl.BlockSpec((dff, d), lambda i: (0, 0), pipeline_mode=pl.Buffered(1)),
            pl.BlockSpec((None, 1, d), lambda i: (layer, 0, 0)),
            pl.BlockSpec((None, 1, d), lambda i: (layer, 0, 0)),
        ],
        out_specs=[pl.BlockSpec((tm, d), lambda i: (i, 0)), pl.BlockSpec((tm, d), lambda i: (i, 0))],
        out_shape=[jax.ShapeDtypeStruct((m, d), F32), jax.ShapeDtypeStruct((m, d), BF16)],
        compiler_params=_cparams(("parallel",)),
        name="down",
    )(act, h, wd, lg, lb)


def _cast_rows(rows, steps):
    for n in range(min(steps, rows // 16), 0, -1):
        if rows % n == 0 and (rows // n) % 16 == 0:
            return rows // n
    return rows


def _largest_tile(n, cap, quantum):
    t = min(cap, n) // quantum * quantum
    while n % t:
        t -= quantum
    return t


def kernel(x, w_in, gla_w_alpha, gla_b_alpha, gla_norm_g, gla_w_o, conv_w, conv_b, lru_w_a, lru_b_a, lru_w_x,
           lru_b_x, lru_lambda, rnn_w_o, w_out, ln1_g, ln1_b, w_gate_up, w_down, ln2_g, ln2_b):
    batch, seq, d = x.shape
    depth = w_in.shape[0]
    m = batch * seq
    key_w = gla_w_alpha.shape[2]
    val_w = gla_w_o.shape[1]
    lru_w = conv_w.shape[2]
    dff = w_down.shape[1]
    alpha = (2.0 * depth) ** 0.25

    lr0 = 2 * key_w + 2 * val_w
    off = {"q": 0, "k": key_w, "v": 2 * key_w, "g": 2 * key_w + val_w,
           "rx": lr0, "ry": lr0 + lru_w, "ga": lr0 + 2 * lru_w, "gb": lr0 + 2 * lru_w + d}
    n_main = lr0 + 2 * lru_w + 2 * d

    tm_proj = _largest_tile(seq, 1024, 256)
    tn_proj = _largest_tile(n_main, 2048, 512)
    tc_gla = _largest_tile(seq, 1024, GLA_BLOCK)
    tc_lru = _largest_tile(seq, 1024, SUBLANES)
    tm_post = _largest_tile(seq, 256, 128)
    tm_gu = _largest_tile(m, 2048, 256)
    tf_gu = _largest_tile(dff, 512, 256)
    tm_down = _largest_tile(seq, 256, 128)
    ct_regroup = _largest_tile(lr0, 1024, LANES)
    w_in_t = jnp.swapaxes(w_in, 1, 2)

    row3 = lambda p: p[:, None, :]
    w_alpha = jnp.pad(gla_w_alpha, ((0, 0), (0, LANES - GLA_LOW_RANK), (0, 0))).astype(BF16)
    b_alpha, norm_g = row3(gla_b_alpha), row3(gla_norm_g)
    cb, ba, bx, lam = row3(conv_b), row3(lru_b_a), row3(lru_b_x), row3(lru_lambda)
    wa_b, wx_b = lru_w_a.astype(BF16), lru_w_x.astype(BF16)
    l1g, l1b, l2g, l2b = row3(ln1_g), row3(ln1_b), row3(ln2_g), row3(ln2_b)

    h = x.reshape(m, d)
    hb = h
    for l in range(depth):
        w_main, w_lr = _regroup(w_in_t, l, lr0, ct_regroup)
        proj, lr, woa_b, wob_b, wout_b = _in_proj(hb, w_main, w_lr, (gla_w_o, rnn_w_o, w_out), l, tm_proj, tn_proj)
        za = _gla(proj, lr, w_alpha, b_alpha, norm_g, l, batch, seq, tc_gla, off)
        zb = _lru(proj, conv_w, cb, wa_b, ba, wx_b, bx, lam, l, batch, seq, tc_lru, 4 * LRU_BLOCK, off)
        h, hb = _post(za, zb, proj, h, woa_b, wob_b, wout_b, l1g, l1b, l, tm_post, off, alpha)
        act, wd_b = _gate_up(hb, w_gate_up, w_down, l, tm_gu, tf_gu)
        h, hb = _down(act, h, wd_b, l2g, l2b, l, tm_down, alpha)
    return h.reshape(batch, seq, d)
```

```python
import functools

import jax
import jax.numpy as jnp
from jax import lax
from jax.experimental import pallas as pl
from jax.experimental.pallas import tpu as pltpu

F32 = jnp.float32
BF16 = jnp.bfloat16

GLA_HEADS = 4
GLA_LOW_RANK = 16
GLA_TAU = 16.0
GLA_CHUNK = 64
GLA_BLOCK = 256
GLA_NORM_EPS = 1e-5
LRU_BLOCK = 256
LRU_C = 8.0
CONV_WIDTH = 4
LN_EPS = 1e-5
GELU_C0 = 0.7978845608028654
GELU_C1 = 0.044715 * GELU_C0
GATE_UP_ROW_CHUNK = 512
RESIDENT_ROW_CHUNK = 128

LANES = 128
SUBLANES = 8
VMEM_LIMIT_BYTES = 56 * 1024 * 1024
F32_TINY = 1e-30


def _cparams(sem):
    return pltpu.CompilerParams(dimension_semantics=sem, vmem_limit_bytes=VMEM_LIMIT_BYTES)


def _dot(a, b):
    return jnp.dot(a, b, preferred_element_type=F32)


def _dot_nt(a, b):
    return lax.dot_general(a, b, (((1,), (1,)), ((), ())), preferred_element_type=F32)


def _dot_tn(a, b):
    return lax.dot_general(a, b, (((0,), (0,)), ((), ())), preferred_element_type=F32)


def _layer_norm(z, g, b):
    mu = jnp.mean(z, axis=-1, keepdims=True)
    zc = z - mu
    var = jnp.mean(zc * zc, axis=-1, keepdims=True)
    return zc * lax.rsqrt(var + LN_EPS) * g + b


def _regroup_kernel(a_ref, b_ref, c_ref, o_ref, lr_ref, *, gap_tile):
    j = pl.program_id(0)
    d = a_ref.shape[1]

    @pl.when(j < gap_tile)
    def _():
        o_ref[...] = a_ref[...].T.astype(BF16)

    @pl.when(j >= gap_tile)
    def _():
        src = jnp.concatenate([a_ref[GLA_LOW_RANK:, :], b_ref[...]], axis=0)
        o_ref[...] = src.T.astype(BF16)

    @pl.when(j == 0)
    def _():
        rows = jnp.concatenate([c_ref[...], jnp.zeros((LANES - GLA_LOW_RANK, d), F32)], axis=0)
        lr_ref[...] = rows.T.astype(BF16)


def _regroup(w_in_t, layer, lr0, ct):
    _, n_in, d = w_in_t.shape
    n_main = n_in - GLA_LOW_RANK
    assert lr0 % ct == 0 and n_main % ct == 0 and ct % GLA_LOW_RANK == 0
    sub = ct // GLA_LOW_RANK
    return pl.pallas_call(
        functools.partial(_regroup_kernel, gap_tile=lr0 // ct),
        grid=(n_main // ct,),
        in_specs=[
            pl.BlockSpec((None, ct, d), lambda j: (layer, j, 0)),
            pl.BlockSpec((None, GLA_LOW_RANK, d), lambda j: (layer, (j + 1) * sub, 0)),
            pl.BlockSpec((None, GLA_LOW_RANK, d), lambda j: (layer, lr0 // GLA_LOW_RANK, 0)),
        ],
        out_specs=[pl.BlockSpec((d, ct), lambda j: (0, j)), pl.BlockSpec((d, LANES), lambda j: (0, 0))],
        out_shape=[jax.ShapeDtypeStruct((d, n_main), BF16), jax.ShapeDtypeStruct((d, LANES), BF16)],
        compiler_params=_cparams(("arbitrary",)),
        name="regroup",
    )(w_in_t, w_in_t, w_in_t)


def _in_proj_kernel(x_ref, w_ref, wlr_ref, c0_ref, c1_ref, c2_ref, o_ref, lr_ref, b0_ref, b1_ref, b2_ref,
                    xb_ref, *, n_cast, j_ry):
    j = pl.program_id(1)

    @pl.when(j == 0)
    def _():
        xb = x_ref[...].astype(BF16)
        xb_ref[...] = xb
        lr_ref[...] = _dot(xb, wlr_ref[...]).astype(BF16)

    @pl.when(pl.program_id(0) * pl.num_programs(1) + j < n_cast)
    def _():
        b0_ref[...] = c0_ref[...].astype(BF16)
        b1_ref[...] = c1_ref[...].astype(BF16)
        b2_ref[...] = c2_ref[...].astype(BF16)

    @pl.when(j == j_ry)
    def _():
        r = _dot(xb_ref[...], w_ref[...])
        th = jnp.tanh(r * (GELU_C0 + GELU_C1 * (r * r)))
        o_ref[...] = ((0.5 * r) * (1.0 + th)).astype(BF16)

    @pl.when(j != j_ry)
    def _():
        o_ref[...] = _dot(xb_ref[...], w_ref[...]).astype(BF16)


def _in_proj(x, w, wlr, cast_ws, layer, tm, tn, off):
    m, d = x.shape
    n = w.shape[1]
    nj = n // tn
    assert off["ry"] % tn == 0 and off["ga"] - off["ry"] == tn
    rows = cast_ws[0].shape[1]
    rc = _cast_rows(rows, (m // tm) * nj)
    n_cast = rows // rc
    chunk = lambda i, j: jnp.minimum(i * nj + j, n_cast - 1)
    cast_in = lambda cw: pl.BlockSpec((None, rc, cw.shape[2]), lambda i, j: (layer, chunk(i, j), 0))
    cast_out = lambda cw: pl.BlockSpec((rc, cw.shape[2]), lambda i, j: (chunk(i, j), 0))
    return pl.pallas_call(
        functools.partial(_in_proj_kernel, n_cast=n_cast, j_ry=off["ry"] // tn),
        grid=(m // tm, nj),
        in_specs=[
            pl.BlockSpec((tm, d), lambda i, j: (i, 0)),
            pl.BlockSpec((d, tn), lambda i, j: (0, j)),
            pl.BlockSpec((d, LANES), lambda i, j: (0, 0)),
        ] + [cast_in(cw) for cw in cast_ws],
        out_specs=[
            pl.BlockSpec((tm, tn), lambda i, j: (i, j)),
            pl.BlockSpec((tm, LANES), lambda i, j: (i, 0)),
        ] + [cast_out(cw) for cw in cast_ws],
        out_shape=[jax.ShapeDtypeStruct((m, n), BF16), jax.ShapeDtypeStruct((m, LANES), BF16)]
        + [jax.ShapeDtypeStruct(cw.shape[1:], BF16) for cw in cast_ws],
        scratch_shapes=[pltpu.VMEM((tm, d), BF16)],
        compiler_params=_cparams(("arbitrary", "arbitrary")),
        name="in_proj",
    )(x, w, wlr, *cast_ws)


def _gla_kernel(q_ref, k_ref, v_ref, g_ref, lr_ref, wa_ref, ba_ref, ng_ref, o_ref, s_ref, *, scale):
    tc, dk = q_ref.shape
    dv = v_ref.shape[1]
    c, blk = GLA_CHUNK, GLA_BLOCK
    nc = blk // c

    @pl.when(pl.program_id(2) == 0)
    def _():
        s_ref[...] = jnp.zeros_like(s_ref)

    alpha_pre = _dot(lr_ref[...], wa_ref[...]) + ba_ref[...]
    log_a = (jnp.minimum(alpha_pre, 0.0) - jnp.log1p(jnp.exp(-jnp.abs(alpha_pre)))) * (1.0 / GLA_TAU)

    row = lax.broadcasted_iota(jnp.int32, (blk, blk), 0)
    col = lax.broadcasted_iota(jnp.int32, (blk, blk), 1)
    lower = jnp.where(row >= col, 1.0, 0.0)
    tri = jnp.where((row // c) == (col // c), lower, 0.0).astype(BF16)
    zeros_chunk = jnp.zeros((c, dk), BF16)

    def rowsum(vs):
        out = None
        for v in vs:
            out = v if out is None else out + v
        return out

    def scaled(x, e):
        return x if e is None else x * jnp.exp(e)

    for bi in range(tc // blk):
        bs = slice(bi * blk, (bi + 1) * blk)
        la = log_a[bs]
        hi = la.astype(BF16)
        mid = (la - hi.astype(F32)).astype(BF16)
        r = _dot(tri, jnp.concatenate([hi, mid], axis=1))
        bc = r[:, :dk] + r[:, dk:]
        qf = q_ref[bs, :].astype(F32) * scale
        kf = k_ref[bs, :].astype(F32)
        vb = v_ref[bs, :]
        q_dec = qf * jnp.exp(bc)
        k_inv = (kf * jnp.exp(-bc)).astype(BF16)

        cs = [slice(j * c, (j + 1) * c) for j in range(nc)]
        bl = [bc[j * c + c - 1:j * c + c, :] for j in range(nc)]
        k_end = [kf[cs[j]] * jnp.exp(bl[j] - bc[cs[j]]) for j in range(nc)]

        strips = []
        for i in range(nc):
            rows = [scaled(k_end[j], rowsum(bl[j + 1:i])).astype(BF16) for j in range(i)]
            rows.append(k_inv[cs[i]])
            if (c * (i + 1)) % LANES:
                rows.append(zeros_chunk)
            keys_i = jnp.concatenate(rows, axis=0)
            w = keys_i.shape[0]
            sc = _dot_nt(q_dec[cs[i]].astype(BF16), keys_i)
            rr = lax.broadcasted_iota(jnp.int32, (c, w), 0)
            cc = lax.broadcasted_iota(jnp.int32, (c, w), 1)
            sc = jnp.where(cc <= rr + c * i, sc, 0.0).astype(BF16)
            if w < blk:
                sc = jnp.concatenate([sc, jnp.zeros((c, blk - w), BF16)], axis=1)
            strips.append(sc)
        p = jnp.concatenate(strips, axis=0)
        q_anch = jnp.concatenate(
            [scaled(q_dec[cs[i]], rowsum(bl[:i])).astype(BF16) for i in range(nc)], axis=0)
        k_anch = jnp.concatenate(
            [scaled(k_end[j], rowsum(bl[j + 1:])).astype(BF16) for j in range(nc)], axis=0)

        s = s_ref[...]
        o = _dot(jnp.concatenate([p, q_anch], axis=1),
                 jnp.concatenate([vb, s.astype(BF16)], axis=0))
        decay = jnp.exp(rowsum(bl))
        decay_t = jnp.transpose(jnp.broadcast_to(decay, (LANES, dk)))
        s_ref[...] = s * jnp.concatenate([decay_t] * (dv // LANES), axis=1) + _dot_tn(k_anch, vb)

        ms = jnp.mean(o * o, axis=-1, keepdims=True)
        y = o * lax.rsqrt(ms + GLA_NORM_EPS) * ng_ref[...]
        g = g_ref[bs, :].astype(F32)
        o_ref[bs, :] = (y * (g * jax.nn.sigmoid(g))).astype(BF16)


def _gla(proj, lr, wa, ba, ng, layer, batch, seq, tc, off):
    m = proj.shape[0]
    dk = wa.shape[2] // GLA_HEADS
    dv = ng.shape[2]
    nt = seq // tc
    rowmap = lambda b, h, t: b * nt + t
    return pl.pallas_call(
        functools.partial(_gla_kernel, scale=float(dk) ** -0.5),
        grid=(batch, GLA_HEADS, nt),
        in_specs=[
            pl.BlockSpec((tc, dk), lambda b, h, t: (rowmap(b, h, t), off["q"] // dk + h)),
            pl.BlockSpec((tc, dk), lambda b, h, t: (rowmap(b, h, t), off["k"] // dk + h)),
            pl.BlockSpec((tc, dv), lambda b, h, t: (rowmap(b, h, t), off["v"] // dv + h)),
            pl.BlockSpec((tc, dv), lambda b, h, t: (rowmap(b, h, t), off["g"] // dv + h)),
            pl.BlockSpec((tc, LANES), lambda b, h, t: (rowmap(b, h, t), 0)),
            pl.BlockSpec((None, LANES, dk), lambda b, h, t: (layer, 0, h)),
            pl.BlockSpec((None, 1, dk), lambda b, h, t: (layer, 0, h)),
            pl.BlockSpec((None, 1, dv), lambda b, h, t: (layer, 0, 0)),
        ],
        out_specs=pl.BlockSpec((tc, dv), lambda b, h, t: (rowmap(b, h, t), h)),
        out_shape=jax.ShapeDtypeStruct((m, GLA_HEADS * dv), BF16),
        scratch_shapes=[pltpu.VMEM((dk, dv), F32)],
        compiler_params=_cparams(("parallel", "parallel", "arbitrary")),
        name="gla",
    )(proj, proj, proj, proj, lr, wa, ba, ng)


def _lru_kernel(rx_ref, gy_ref, cw_ref, cb_ref, wa_ref, ba_ref, wx_ref, bx_ref, lam_ref, o_ref,
                xbuf_ref, a_ref, u_ref, hc_ref):
    tc, wc = rx_ref.shape
    gb = wc // LRU_BLOCK
    t = pl.program_id(2)

    @pl.when(t == 0)
    def _():
        xbuf_ref[0:SUBLANES, :] = jnp.zeros((SUBLANES, wc), F32)
        hc_ref[...] = jnp.zeros_like(hc_ref)

    @pl.when(t > 0)
    def _():
        xbuf_ref[0:SUBLANES, :] = xbuf_ref[tc:tc + SUBLANES, :]

    xbuf_ref[SUBLANES:SUBLANES + tc, :] = rx_ref[...].astype(F32)

    hx = cb_ref[...] + cw_ref[CONV_WIDTH - 1:CONV_WIDTH, :] * xbuf_ref[SUBLANES:SUBLANES + tc, :]
    for j in range(CONV_WIDTH - 1):
        start = SUBLANES - (CONV_WIDTH - 1) + j
        hx = hx + cw_ref[j:j + 1, :] * xbuf_ref[start:start + tc, :]

    lam = lam_ref[...]
    neg_c_softplus = -LRU_C * (jnp.maximum(-lam, 0.0) + jnp.log1p(jnp.exp(-jnp.abs(lam))))
    for gi in range(gb):
        cs = slice(gi * LRU_BLOCK, (gi + 1) * LRU_BLOCK)
        hxg = hx[:, cs]
        hxb = hxg.astype(BF16)
        r = jax.nn.sigmoid(_dot(hxb, wa_ref[gi]) + ba_ref[:, cs])
        i = jax.nn.sigmoid(_dot(hxb, wx_ref[gi]) + bx_ref[:, cs])
        a = jnp.exp(neg_c_softplus[:, cs] * r)
        y = 1.0 - a * a
        a_ref[:, cs] = a
        u_ref[:, cs] = (y * lax.rsqrt(jnp.maximum(y, F32_TINY))) * (i * hxg)

    row = lax.broadcasted_iota(jnp.int32, (SUBLANES, wc), 0)

    def body(s, carry):
        r0 = pl.multiple_of(s * SUBLANES, SUBLANES)
        a8 = a_ref[pl.ds(r0, SUBLANES), :]
        u8 = u_ref[pl.ds(r0, SUBLANES), :]
        for d in (1, 2, 4):
            keep = row >= d
            a_sh = pltpu.roll(a8, d, 0)
            u_sh = pltpu.roll(u8, d, 0)
            u8 = jnp.where(keep, a8 * u_sh + u8, u8)
            a8 = jnp.where(keep, a8 * a_sh, a8)
        h8 = u8 + a8 * carry
        u_ref[pl.ds(r0, SUBLANES), :] = h8
        return h8[SUBLANES - 1:SUBLANES, :]

    hc_ref[...] = lax.fori_loop(0, tc // SUBLANES, body, hc_ref[...], unroll=4)

    o_ref[...] = (u_ref[...] * gy_ref[...].astype(F32)).astype(BF16)


def _lru(proj, cw, cb, wa, ba, wx, bx, lam, layer, batch, seq, tc, wc, off):
    m = proj.shape[0]
    width = cw.shape[2]
    gb = wc // LRU_BLOCK
    nt = seq // tc
    rowmap = lambda b, n, t: b * nt + t
    vec = lambda: pl.BlockSpec((None, 1, wc), lambda b, n, t: (layer, 0, n))
    gate_w = lambda: pl.BlockSpec((None, gb, LRU_BLOCK, LRU_BLOCK), lambda b, n, t: (layer, n, 0, 0))
    return pl.pallas_call(
        _lru_kernel,
        grid=(batch, width // wc, nt),
        in_specs=[
            pl.BlockSpec((tc, wc), lambda b, n, t: (rowmap(b, n, t), off["rx"] // wc + n)),
            pl.BlockSpec((tc, wc), lambda b, n, t: (rowmap(b, n, t), off["ry"] // wc + n)),
            pl.BlockSpec((None, CONV_WIDTH, wc), lambda b, n, t: (layer, 0, n)),
            vec(), gate_w(), vec(), gate_w(), vec(), vec(),
        ],
        out_specs=pl.BlockSpec((tc, wc), lambda b, n, t: (rowmap(b, n, t), n)),
        out_shape=jax.ShapeDtypeStruct((m, width), BF16),
        scratch_shapes=[
            pltpu.VMEM((tc + SUBLANES, wc), F32),
            pltpu.VMEM((tc, wc), F32),
            pltpu.VMEM((tc, wc), F32),
            pltpu.VMEM((1, wc), F32),
        ],
        compiler_params=_cparams(("parallel", "parallel", "arbitrary")),
        name="lru",
    )(proj, proj, cw, cb, wa, ba, wx, bx, lam)


def _post_kernel(za_ref, zb_ref, ga_ref, gb_ref, h_ref, wa_ref, wb_ref, wo_ref, lg_ref, lb_ref,
                 o_ref, ob_ref, *, alpha):
    tm = za_ref.shape[0]
    mc = min(tm, RESIDENT_ROW_CHUNK)
    for mi in range(tm // mc):
        rs = slice(mi * mc, (mi + 1) * mc)
        ya = _dot(za_ref[rs, :], wa_ref[...])
        yb = _dot(zb_ref[rs, :], wb_ref[...])
        merged = (jax.nn.sigmoid(ga_ref[rs, :].astype(F32)) * ya
                  + jax.nn.sigmoid(gb_ref[rs, :].astype(F32)) * yb)
        mix = _dot(merged.astype(BF16), wo_ref[...])
        h1 = _layer_norm(alpha * h_ref[rs, :] + mix, lg_ref[...], lb_ref[...])
        o_ref[rs, :] = h1
        ob_ref[rs, :] = h1.astype(BF16)


def _post(za, zb, proj, h, woa, wob, wout, lg, lb, layer, tm, off, alpha):
    m, d = h.shape
    act = lambda col: pl.BlockSpec((tm, d), lambda i: (i, col))
    resident = lambda w: pl.BlockSpec(w.shape, lambda i: (0, 0), pipeline_mode=pl.Buffered(1))
    vec = lambda: pl.BlockSpec((None, 1, d), lambda i: (layer, 0, 0))
    return pl.pallas_call(
        functools.partial(_post_kernel, alpha=alpha),
        grid=(m // tm,),
        in_specs=[act(0), act(0), act(off["ga"] // d), act(off["gb"] // d), act(0),
                  resident(woa), resident(wob), resident(wout), vec(), vec()],
        out_specs=[act(0), act(0)],
        out_shape=[jax.ShapeDtypeStruct((m, d), F32), jax.ShapeDtypeStruct((m, d), BF16)],
        compiler_params=_cparams(("parallel",)),
        name="post",
    )(za, zb, proj, proj, h, woa, wob, wout, lg, lb)


def _gate_up_kernel(x_ref, wg_ref, wu_ref, c_ref, o_ref, b_ref, w_s, *, n_cast):
    tf = wg_ref.shape[1]
    i = pl.program_id(1)

    @pl.when(i == 0)
    def _():
        w_s[:, :tf] = wg_ref[...].astype(BF16)
        w_s[:, tf:] = wu_ref[...].astype(BF16)

    @pl.when(pl.program_id(0) * pl.num_programs(1) + i < n_cast)
    def _():
        b_ref[...] = c_ref[...].astype(BF16)

    tm = x_ref.shape[0]
    mc = min(tm, GATE_UP_ROW_CHUNK)
    for mi in range(tm // mc):
        rs = slice(mi * mc, (mi + 1) * mc)
        r = _dot(x_ref[rs, :], w_s[...])
        gate, up = r[:, :tf], r[:, tf:]
        o_ref[rs, :] = (gate * jax.nn.sigmoid(gate) * up).astype(BF16)


def _gate_up(xb, wgu, wd, layer, tm, tf):
    m, d = xb.shape
    dff = wgu.shape[2] // 2
    nf = dff // tf
    ni = m // tm
    rc = _cast_rows(dff, nf * ni)
    n_cast = dff // rc
    chunk = lambda j, i: jnp.minimum(j * ni + i, n_cast - 1)
    return pl.pallas_call(
        functools.partial(_gate_up_kernel, n_cast=n_cast),
        grid=(nf, ni),
        in_specs=[
            pl.BlockSpec((tm, d), lambda j, i: (i, 0)),
            pl.BlockSpec((None, d, tf), lambda j, i: (layer, 0, j)),
            pl.BlockSpec((None, d, tf), lambda j, i: (layer, 0, nf + j)),
            pl.BlockSpec((None, rc, d), lambda j, i: (layer, chunk(j, i), 0)),
        ],
        out_specs=[pl.BlockSpec((tm, tf), lambda j, i: (i, j)),
                   pl.BlockSpec((rc, d), lambda j, i: (chunk(j, i), 0))],
        out_shape=[jax.ShapeDtypeStruct((m, dff), BF16), jax.ShapeDtypeStruct((dff, d), BF16)],
        scratch_shapes=[pltpu.VMEM((d, 2 * tf), BF16)],
        compiler_params=_cparams(("arbitrary", "arbitrary")),
        name="gate_up",
    )(xb, wgu, wgu, wd)


def _down_kernel(a_ref, h_ref, wd_ref, lg_ref, lb_ref, o_ref, *, alpha):
    tm = a_ref.shape[0]
    mc = min(tm, RESIDENT_ROW_CHUNK)
    for mi in range(tm // mc):
        rs = slice(mi * mc, (mi + 1) * mc)
        y = _dot(a_ref[rs, :], wd_ref[...])
        h2 = _layer_norm(alpha * h_ref[rs, :] + y, lg_ref[...], lb_ref[...])
        o_ref[rs, :] = h2


def _down(act, h, wd, lg, lb, layer, tm, alpha):
    m, d = h.shape
    dff = wd.shape[0]
    return pl.pallas_call(
        functools.partial(_down_kernel, alpha=alpha),
        grid=(m // tm,),
        in_specs=[
            pl.BlockSpec((tm, dff), lambda i: (i, 0)),
            pl.BlockSpec((tm, d), lambda i: (i, 0)),
            pl.BlockSpec((dff, d), lambda i: (0, 0), pipeline_mode=pl.Buffered(1)),
            pl.BlockSpec((None, 1, d), lambda i: (layer, 0, 0)),
            pl.BlockSpec((None, 1, d), lambda i: (layer, 0, 0)),
        ],
        out_specs=pl.BlockSpec((tm, d), lambda i: (i, 0)),
        out_shape=jax.ShapeDtypeStruct((m, d), F32),
        compiler_params=_cparams(("parallel",)),
        name="down",
    )(act, h, wd, lg, lb)


def _cast_rows(rows, steps):
    for n in range(min(steps, rows // 16), 0, -1):
        if rows % n == 0 and (rows // n) % 16 == 0:
            return rows // n
    return rows


def _largest_tile(n, cap, quantum):
    t = min(cap, n) // quantum * quantum
    while n % t:
        t -= quantum
    return t


def kernel(x, w_in, gla_w_alpha, gla_b_alpha, gla_norm_g, gla_w_o, conv_w, conv_b, lru_w_a, lru_b_a, lru_w_x,
           lru_b_x, lru_lambda, rnn_w_o, w_out, ln1_g, ln1_b, w_gate_up, w_down, ln2_g, ln2_b):
    batch, seq, d = x.shape
    depth = w_in.shape[0]
    m = batch * seq
    key_w = gla_w_alpha.shape[2]
    val_w = gla_w_o.shape[1]
    lru_w = conv_w.shape[2]
    dff = w_down.shape[1]
    alpha = (2.0 * depth) ** 0.25

    lr0 = 2 * key_w + 2 * val_w
    off = {"q": 0, "k": key_w, "v": 2 * key_w, "g": 2 * key_w + val_w,
           "rx": lr0, "ry": lr0 + lru_w, "ga": lr0 + 2 * lru_w, "gb": lr0 + 2 * lru_w + d}
    n_main = lr0 + 2 * lru_w + 2 * d

    tm_proj = _largest_tile(seq, 1024, 256)
    tn_proj = _largest_tile(n_main, 2048, 512)
    tc_gla = _largest_tile(seq, 1024, GLA_BLOCK)
    tc_lru = _largest_tile(seq, 1024, SUBLANES)
    tm_post = _largest_tile(seq, 256, 128)
    tm_gu = _largest_tile(m, 2048, 256)
    tf_gu = _largest_tile(dff, 512, 256)
    tm_down = _largest_tile(seq, 512, 128)
    ct_regroup = _largest_tile(lr0, 1024, LANES)
    w_in_t = jnp.swapaxes(w_in, 1, 2)

    row3 = lambda p: p[:, None, :]
    w_alpha = jnp.pad(gla_w_alpha, ((0, 0), (0, LANES - GLA_LOW_RANK), (0, 0))).astype(BF16)
    b_alpha, norm_g = row3(gla_b_alpha), row3(gla_norm_g)
    cb, ba, bx, lam = row3(conv_b), row3(lru_b_a), row3(lru_b_x), row3(lru_lambda)
    wa_b, wx_b = lru_w_a.astype(BF16), lru_w_x.astype(BF16)
    l1g, l1b, l2g, l2b = row3(ln1_g), row3(ln1_b), row3(ln2_g), row3(ln2_b)

    h = x.reshape(m, d)
    for l in range(depth):
        w_main, w_lr = _regroup(w_in_t, l, lr0, ct_regroup)
        proj, lr, woa_b, wob_b, wout_b = _in_proj(h, w_main, w_lr, (gla_w_o, rnn_w_o, w_out), l, tm_proj, tn_proj, off)
        za = _gla(proj, lr, w_alpha, b_alpha, norm_g, l, batch, seq, tc_gla, off)
        zb = _lru(proj, conv_w, cb, wa_b, ba, wx_b, bx, lam, l, batch, seq, tc_lru, 4 * LRU_BLOCK, off)
        h, hb = _post(za, zb, proj, h, woa_b, wob_b, wout_b, l1g, l1b, l, tm_post, off, alpha)
        act, wd_b = _gate_up(hb, w_gate_up, w_down, l, tm_gu, tf_gu)
        h = _down(act, h, wd_b, l2g, l2b, l, tm_down, alpha)
    return h.reshape(batch, seq, d)
```

```python
import functools

import jax
import jax.numpy as jnp
from jax import lax
from jax.experimental import pallas as pl
from jax.experimental.pallas import tpu as pltpu

F32 = jnp.float32
BF16 = jnp.bfloat16

GLA_HEADS = 4
GLA_LOW_RANK = 16
GLA_TAU = 16.0
GLA_CHUNK = 64
GLA_BLOCK = 256
GLA_NORM_EPS = 1e-5
LRU_BLOCK = 256
LRU_C = 8.0
CONV_WIDTH = 4
LN_EPS = 1e-5
GELU_C0 = 0.7978845608028654
GELU_C1 = 0.044715 * GELU_C0
GATE_UP_ROW_CHUNK = 512
RESIDENT_ROW_CHUNK = 128

LANES = 128
SUBLANES = 8
VMEM_LIMIT_BYTES = 56 * 1024 * 1024
F32_TINY = 1e-30


def _cparams(sem):
    return pltpu.CompilerParams(dimension_semantics=sem, vmem_limit_bytes=VMEM_LIMIT_BYTES)


def _dot(a, b):
    return jnp.dot(a, b, preferred_element_type=F32)


def _dot_nt(a, b):
    return lax.dot_general(a, b, (((1,), (1,)), ((), ())), preferred_element_type=F32)


def _dot_tn(a, b):
    return lax.dot_general(a, b, (((0,), (0,)), ((), ())), preferred_element_type=F32)


def _layer_norm(z, g, b):
    mu = jnp.mean(z, axis=-1, keepdims=True)
    zc = z - mu
    var = jnp.mean(zc * zc, axis=-1, keepdims=True)
    return zc * lax.rsqrt(var + LN_EPS) * g + b


def _regroup_kernel(a_ref, b_ref, c_ref, o_ref, lr_ref, *, gap_tile):
    j = pl.program_id(0)
    d = a_ref.shape[1]

    @pl.when(j < gap_tile)
    def _():
        o_ref[...] = a_ref[...].T.astype(BF16)

    @pl.when(j >= gap_tile)
    def _():
        src = jnp.concatenate([a_ref[GLA_LOW_RANK:, :], b_ref[...]], axis=0)
        o_ref[...] = src.T.astype(BF16)

    @pl.when(j == 0)
    def _():
        rows = jnp.concatenate([c_ref[...], jnp.zeros((LANES - GLA_LOW_RANK, d), F32)], axis=0)
        lr_ref[...] = rows.T.astype(BF16)


def _regroup(w_in_t, layer, lr0, ct):
    _, n_in, d = w_in_t.shape
    n_main = n_in - GLA_LOW_RANK
    assert lr0 % ct == 0 and n_main % ct == 0 and ct % GLA_LOW_RANK == 0
    sub = ct // GLA_LOW_RANK
    return pl.pallas_call(
        functools.partial(_regroup_kernel, gap_tile=lr0 // ct),
        grid=(n_main // ct,),
        in_specs=[
            pl.BlockSpec((None, ct, d), lambda j: (layer, j, 0)),
            pl.BlockSpec((None, GLA_LOW_RANK, d), lambda j: (layer, (j + 1) * sub, 0)),
            pl.BlockSpec((None, GLA_LOW_RANK, d), lambda j: (layer, lr0 // GLA_LOW_RANK, 0)),
        ],
        out_specs=[pl.BlockSpec((d, ct), lambda j: (0, j)), pl.BlockSpec((d, LANES), lambda j: (0, 0))],
        out_shape=[jax.ShapeDtypeStruct((d, n_main), BF16), jax.ShapeDtypeStruct((d, LANES), BF16)],
        compiler_params=_cparams(("arbitrary",)),
        name="regroup",
    )(w_in_t, w_in_t, w_in_t)


def _in_proj_kernel(x_ref, w_ref, wlr_ref, c0_ref, c1_ref, c2_ref, o_ref, lr_ref, b0_ref, b1_ref, b2_ref,
                    xb_ref, *, n_cast, j_ry):
    j = pl.program_id(1)

    @pl.when(j == 0)
    def _():
        xb = x_ref[...].astype(BF16)
        xb_ref[...] = xb
        lr_ref[...] = _dot(xb, wlr_ref[...]).astype(BF16)

    @pl.when(pl.program_id(0) * pl.num_programs(1) + j < n_cast)
    def _():
        b0_ref[...] = c0_ref[...].astype(BF16)
        b1_ref[...] = c1_ref[...].astype(BF16)
        b2_ref[...] = c2_ref[...].astype(BF16)

    @pl.when(j == j_ry)
    def _():
        r = _dot(xb_ref[...], w_ref[...])
        th = jnp.tanh(r * (GELU_C0 + GELU_C1 * (r * r)))
        o_ref[...] = ((0.5 * r) * (1.0 + th)).astype(BF16)

    @pl.when(j != j_ry)
    def _():
        o_ref[...] = _dot(xb_ref[...], w_ref[...]).astype(BF16)


def _in_proj(x, w, wlr, cast_ws, layer, tm, tn, off):
    m, d = x.shape
    n = w.shape[1]
    nj = n // tn
    assert off["ry"] % tn == 0 and off["ga"] - off["ry"] == tn
    rows = cast_ws[0].shape[1]
    rc = _cast_rows(rows, (m // tm) * nj)
    n_cast = rows // rc
    chunk = lambda i, j: jnp.minimum(i * nj + j, n_cast - 1)
    cast_in = lambda cw: pl.BlockSpec((None, rc, cw.shape[2]), lambda i, j: (layer, chunk(i, j), 0))
    cast_out = lambda cw: pl.BlockSpec((rc, cw.shape[2]), lambda i, j: (chunk(i, j), 0))
    return pl.pallas_call(
        functools.partial(_in_proj_kernel, n_cast=n_cast, j_ry=off["ry"] // tn),
        grid=(m // tm, nj),
        in_specs=[
            pl.BlockSpec((tm, d), lambda i, j: (i, 0)),
            pl.BlockSpec((d, tn), lambda i, j: (0, j)),
            pl.BlockSpec((d, LANES), lambda i, j: (0, 0)),
        ] + [cast_in(cw) for cw in cast_ws],
        out_specs=[
            pl.BlockSpec((tm, tn), lambda i, j: (i, j)),
            pl.BlockSpec((tm, LANES), lambda i, j: (i, 0)),
        ] + [cast_out(cw) for cw in cast_ws],
        out_shape=[jax.ShapeDtypeStruct((m, n), BF16), jax.ShapeDtypeStruct((m, LANES), BF16)]
        + [jax.ShapeDtypeStruct(cw.shape[1:], BF16) for cw in cast_ws],
        scratch_shapes=[pltpu.VMEM((tm, d), BF16)],
        compiler_params=_cparams(("arbitrary", "arbitrary")),
        name="in_proj",
    )(x, w, wlr, *cast_ws)


def _gla_kernel(q_ref, k_ref, v_ref, g_ref, lr_ref, wa_ref, ba_ref, ng_ref, o_ref, s_ref, *, scale):
    tc, dk = q_ref.shape
    dv = v_ref.shape[1]
    c, blk = GLA_CHUNK, GLA_BLOCK
    nc = blk // c

    @pl.when(pl.program_id(2) == 0)
    def _():
        s_ref[...] = jnp.zeros_like(s_ref)

    alpha_pre = _dot(lr_ref[...], wa_ref[...]) + ba_ref[...]
    log_a = (jnp.minimum(alpha_pre, 0.0) - jnp.log1p(jnp.exp(-jnp.abs(alpha_pre)))) * (1.0 / GLA_TAU)

    row = lax.broadcasted_iota(jnp.int32, (blk, blk), 0)
    col = lax.broadcasted_iota(jnp.int32, (blk, blk), 1)
    lower = jnp.where(row >= col, 1.0, 0.0)
    tri = jnp.where((row // c) == (col // c), lower, 0.0).astype(BF16)
    zeros_chunk = jnp.zeros((c, dk), BF16)

    def rowsum(vs):
        out = None
        for v in vs:
            out = v if out is None else out + v
        return out

    def scaled(x, e):
        return x if e is None else x * jnp.exp(e)

    for bi in range(tc // blk):
        bs = slice(bi * blk, (bi + 1) * blk)
        la = log_a[bs]
        hi = la.astype(BF16)
        mid = (la - hi.astype(F32)).astype(BF16)
        r = _dot(tri, jnp.concatenate([hi, mid], axis=1))
        bc = r[:, :dk] + r[:, dk:]
        qf = q_ref[bs, :].astype(F32) * scale
        kf = k_ref[bs, :].astype(F32)
        vb = v_ref[bs, :]
        q_dec = qf * jnp.exp(bc)
        k_inv = (kf * jnp.exp(-bc)).astype(BF16)

        cs = [slice(j * c, (j + 1) * c) for j in range(nc)]
        bl = [bc[j * c + c - 1:j * c + c, :] for j in range(nc)]
        k_end = [kf[cs[j]] * jnp.exp(bl[j] - bc[cs[j]]) for j in range(nc)]

        strips = []
        for i in range(nc):
            rows = [scaled(k_end[j], rowsum(bl[j + 1:i])).astype(BF16) for j in range(i)]
            rows.append(k_inv[cs[i]])
            if (c * (i + 1)) % LANES:
                rows.append(zeros_chunk)
            keys_i = jnp.concatenate(rows, axis=0)
            w = keys_i.shape[0]
            sc = _dot_nt(q_dec[cs[i]].astype(BF16), keys_i)
            rr = lax.broadcasted_iota(jnp.int32, (c, w), 0)
            cc = lax.broadcasted_iota(jnp.int32, (c, w), 1)
            sc = jnp.where(cc <= rr + c * i, sc, 0.0).astype(BF16)
            if w < blk:
                sc = jnp.concatenate([sc, jnp.zeros((c, blk - w), BF16)], axis=1)
            strips.append(sc)
        p = jnp.concatenate(strips, axis=0)
        q_anch = jnp.concatenate(
            [scaled(q_dec[cs[i]], rowsum(bl[:i])).astype(BF16) for i in range(nc)], axis=0)
        k_anch = jnp.concatenate(
            [scaled(k_end[j], rowsum(bl[j + 1:])).astype(BF16) for j in range(nc)], axis=0)

        s = s_ref[...]
        o = _dot(jnp.concatenate([p, q_anch], axis=1),
                 jnp.concatenate([vb, s.astype(BF16)], axis=0))
        decay = jnp.exp(rowsum(bl))
        decay_t = jnp.transpose(jnp.broadcast_to(decay, (LANES, dk)))
        s_ref[...] = s * jnp.concatenate([decay_t] * (dv // LANES), axis=1) + _dot_tn(k_anch, vb)

        ms = jnp.mean(o * o, axis=-1, keepdims=True)
        y = o * lax.rsqrt(ms + GLA_NORM_EPS) * ng_ref[...]
        g = g_ref[bs, :].astype(F32)
        o_ref[bs, :] = (y * (g * jax.nn.sigmoid(g))).astype(BF16)


def _gla(proj, lr, wa, ba, ng, layer, batch, seq, tc, off):
    m = proj.shape[0]
    dk = wa.shape[2] // GLA_HEADS
    dv = ng.shape[2]
    nt = seq // tc
    rowmap = lambda b, h, t: b * nt + t
    return pl.pallas_call(
        functools.partial(_gla_kernel, scale=float(dk) ** -0.5),
        grid=(batch, GLA_HEADS, nt),
        in_specs=[
            pl.BlockSpec((tc, dk), lambda b, h, t: (rowmap(b, h, t), off["q"] // dk + h)),
            pl.BlockSpec((tc, dk), lambda b, h, t: (rowmap(b, h, t), off["k"] // dk + h)),
            pl.BlockSpec((tc, dv), lambda b, h, t: (rowmap(b, h, t), off["v"] // dv + h)),
            pl.BlockSpec((tc, dv), lambda b, h, t: (rowmap(b, h, t), off["g"] // dv + h)),
            pl.BlockSpec((tc, LANES), lambda b, h, t: (rowmap(b, h, t), 0)),
            pl.BlockSpec((None, LANES, dk), lambda b, h, t: (layer, 0, h)),
            pl.BlockSpec((None, 1, dk), lambda b, h, t: (layer, 0, h)),
            pl.BlockSpec((None, 1, dv), lambda b, h, t: (layer, 0, 0)),
        ],
        out_specs=pl.BlockSpec((tc, dv), lambda b, h, t: (rowmap(b, h, t), h)),
        out_shape=jax.ShapeDtypeStruct((m, GLA_HEADS * dv), BF16),
        scratch_shapes=[pltpu.VMEM((dk, dv), F32)],
        compiler_params=_cparams(("parallel", "parallel", "arbitrary")),
        name="gla",
    )(proj, proj, proj, proj, lr, wa, ba, ng)


def _lru_kernel(rx_ref, gy_ref, cw_ref, cb_ref, wa_ref, ba_ref, wx_ref, bx_ref, lam_ref, o_ref,
                xbuf_ref, a_ref, u_ref, hc_ref):
    tc, wc = rx_ref.shape
    gb = wc // LRU_BLOCK
    t = pl.program_id(2)

    @pl.when(t == 0)
    def _():
        xbuf_ref[0:SUBLANES, :] = jnp.zeros((SUBLANES, wc), F32)
        hc_ref[...] = jnp.zeros_like(hc_ref)

    @pl.when(t > 0)
    def _():
        xbuf_ref[0:SUBLANES, :] = xbuf_ref[tc:tc + SUBLANES, :]

    xbuf_ref[SUBLANES:SUBLANES + tc, :] = rx_ref[...].astype(F32)

    hx = cb_ref[...] + cw_ref[CONV_WIDTH - 1:CONV_WIDTH, :] * xbuf_ref[SUBLANES:SUBLANES + tc, :]
    for j in range(CONV_WIDTH - 1):
        start = SUBLANES - (CONV_WIDTH - 1) + j
        hx = hx + cw_ref[j:j + 1, :] * xbuf_ref[start:start + tc, :]

    lam = lam_ref[...]
    neg_c_softplus = -LRU_C * (jnp.maximum(-lam, 0.0) + jnp.log1p(jnp.exp(-jnp.abs(lam))))
    for gi in range(gb):
        cs = slice(gi * LRU_BLOCK, (gi + 1) * LRU_BLOCK)
        hxg = hx[:, cs]
        hxb = hxg.astype(BF16)
        r = jax.nn.sigmoid(_dot(hxb, wa_ref[gi]) + ba_ref[:, cs])
        i = jax.nn.sigmoid(_dot(hxb, wx_ref[gi]) + bx_ref[:, cs])
        a = jnp.exp(neg_c_softplus[:, cs] * r)
        y = 1.0 - a * a
        a_ref[:, cs] = a
        u_ref[:, cs] = (y * lax.rsqrt(jnp.maximum(y, F32_TINY))) * (i * hxg)

    row = lax.broadcasted_iota(jnp.int32, (SUBLANES, wc), 0)

    def body(s, carry):
        r0 = pl.multiple_of(s * SUBLANES, SUBLANES)
        a8 = a_ref[pl.ds(r0, SUBLANES), :]
        u8 = u_ref[pl.ds(r0, SUBLANES), :]
        for d in (1, 2, 4):
            keep = row >= d
            a_sh = pltpu.roll(a8, d, 0)
            u_sh = pltpu.roll(u8, d, 0)
            u8 = jnp.where(keep, a8 * u_sh + u8, u8)
            a8 = jnp.where(keep, a8 * a_sh, a8)
        h8 = u8 + a8 * carry
        u_ref[pl.ds(r0, SUBLANES), :] = h8
        return h8[SUBLANES - 1:SUBLANES, :]

    hc_ref[...] = lax.fori_loop(0, tc // SUBLANES, body, hc_ref[...], unroll=4)

    o_ref[...] = (u_ref[...] * gy_ref[...].astype(F32)).astype(BF16)


def _lru(proj, cw, cb, wa, ba, wx, bx, lam, layer, batch, seq, tc, wc, off):
    m = proj.shape[0]
    width = cw.shape[2]
    gb = wc // LRU_BLOCK
    nt = seq // tc
    rowmap = lambda b, n, t: b * nt + t
    vec = lambda: pl.BlockSpec((None, 1, wc), lambda b, n, t: (layer, 0, n))
    gate_w = lambda: pl.BlockSpec((None, gb, LRU_BLOCK, LRU_BLOCK), lambda b, n, t: (layer, n, 0, 0))
    return pl.pallas_call(
        _lru_kernel,
        grid=(batch, width // wc, nt),
        in_specs=[
            pl.BlockSpec((tc, wc), lambda b, n, t: (rowmap(b, n, t), off["rx"] // wc + n)),
            pl.BlockSpec((tc, wc), lambda b, n, t: (rowmap(b, n, t), off["ry"] // wc + n)),
            pl.BlockSpec((None, CONV_WIDTH, wc), lambda b, n, t: (layer, 0, n)),
            vec(), gate_w(), vec(), gate_w(), vec(), vec(),
        ],
        out_specs=pl.BlockSpec((tc, wc), lambda b, n, t: (rowmap(b, n, t), n)),
        out_shape=jax.ShapeDtypeStruct((m, width), BF16),
        scratch_shapes=[
            pltpu.VMEM((tc + SUBLANES, wc), F32),
            pltpu.VMEM((tc, wc), F32),
            pltpu.VMEM((tc, wc), F32),
            pltpu.VMEM((1, wc), F32),
        ],
        compiler_params=_cparams(("parallel", "parallel", "arbitrary")),
        name="lru",
    )(proj, proj, cw, cb, wa, ba, wx, bx, lam)


def _post_kernel(za_ref, zb_ref, ga_ref, gb_ref, h_ref, wa_ref, wb_ref, wo_ref, lg_ref, lb_ref,
                 o_ref, ob_ref, *, alpha):
    tm = za_ref.shape[0]
    mc = min(tm, RESIDENT_ROW_CHUNK)
    for mi in range(tm // mc):
        rs = slice(mi * mc, (mi + 1) * mc)
        ya = _dot(za_ref[rs, :], wa_ref[...])
        yb = _dot(zb_ref[rs, :], wb_ref[...])
        merged = (jax.nn.sigmoid(ga_ref[rs, :].astype(F32)) * ya
                  + jax.nn.sigmoid(gb_ref[rs, :].astype(F32)) * yb)
        mix = _dot(merged.astype(BF16), wo_ref[...])
        h1 = _layer_norm(alpha * h_ref[rs, :] + mix, lg_ref[...], lb_ref[...])
        o_ref[rs, :] = h1
        ob_ref[rs, :] = h1.astype(BF16)


def _post(za, zb, proj, h, woa, wob, wout, lg, lb, layer, tm, off, alpha):
    m, d = h.shape
    act = lambda col: pl.BlockSpec((tm, d), lambda i: (i, col))
    resident = lambda w: pl.BlockSpec(w.shape, lambda i: (0, 0), pipeline_mode=pl.Buffered(1))
    vec = lambda: pl.BlockSpec((None, 1, d), lambda i: (layer, 0, 0))
    return pl.pallas_call(
        functools.partial(_post_kernel, alpha=alpha),
        grid=(m // tm,),
        in_specs=[act(0), act(0), act(off["ga"] // d), act(off["gb"] // d), act(0),
                  resident(woa), resident(wob), resident(wout), vec(), vec()],
        out_specs=[act(0), act(0)],
        out_shape=[jax.ShapeDtypeStruct((m, d), F32), jax.ShapeDtypeStruct((m, d), BF16)],
        compiler_params=_cparams(("parallel",)),
        name="post",
    )(za, zb, proj, proj, h, woa, wob, wout, lg, lb)


def _gate_up_kernel(x_ref, wg_ref, wu_ref, c_ref, o_ref, b_ref, w_s, *, n_cast):
    tf = wg_ref.shape[1]
    i = pl.program_id(1)

    @pl.when(i == 0)
    def _():
        w_s[:, :tf] = wg_ref[...].astype(BF16)
        w_s[:, tf:] = wu_ref[...].astype(BF16)

    @pl.when(pl.program_id(0) * pl.num_programs(1) + i < n_cast)
    def _():
        b_ref[...] = c_ref[...].astype(BF16)

    tm = x_ref.shape[0]
    mc = min(tm, GATE_UP_ROW_CHUNK)
    for mi in range(tm // mc):
        rs = slice(mi * mc, (mi + 1) * mc)
        r = _dot(x_ref[rs, :], w_s[...])
        gate, up = r[:, :tf], r[:, tf:]
        o_ref[rs, :] = (gate * jax.nn.sigmoid(gate) * up).astype(BF16)


def _gate_up(xb, wgu, wd, layer, tm, tf):
    m, d = xb.shape
    dff = wgu.shape[2] // 2
    nf = dff // tf
    ni = m // tm
    rc = _cast_rows(dff, nf * ni)
    n_cast = dff // rc
    chunk = lambda j, i: jnp.minimum(j * ni + i, n_cast - 1)
    return pl.pallas_call(
        functools.partial(_gate_up_kernel, n_cast=n_cast),
        grid=(nf, ni),
        in_specs=[
            pl.BlockSpec((tm, d), lambda j, i: (i, 0)),
            pl.BlockSpec((None, d, tf), lambda j, i: (layer, 0, j)),
            pl.BlockSpec((None, d, tf), lambda j, i: (layer, 0, nf + j)),
            pl.BlockSpec((None, rc, d), lambda j, i: (layer, chunk(j, i), 0)),
        ],
        out_specs=[pl.BlockSpec((tm, tf), lambda j, i: (i, j)),
                   pl.BlockSpec((rc, d), lambda j, i: (chunk(j, i), 0))],
        out_shape=[jax.ShapeDtypeStruct((m, dff), BF16), jax.ShapeDtypeStruct((dff, d), BF16)],
        scratch_shapes=[pltpu.VMEM((d, 2 * tf), BF16)],
        compiler_params=_cparams(("arbitrary", "arbitrary")),
        name="gate_up",
    )(xb, wgu, wgu, wd)


def _down_kernel(a_ref, h_ref, wd_ref, lg_ref, lb_ref, o_ref, *, alpha):
    tm = a_ref.shape[0]
    mc = min(tm, RESIDENT_ROW_CHUNK)
    for mi in range(tm // mc):
        rs = slice(mi * mc, (mi + 1) * mc)
        y = _dot(a_ref[rs, :], wd_ref[...])
        h2 = _layer_norm(alpha * h_ref[rs, :] + y, lg_ref[...], lb_ref[...])
        o_ref[rs, :] = h2


def _down(act, h, wd, lg, lb, layer, tm, alpha):
    m, d = h.shape
    dff = wd.shape[0]
    return pl.pallas_call(
        functools.partial(_down_kernel, alpha=alpha),
        grid=(m // tm,),
        in_specs=[
            pl.BlockSpec((tm, dff), lambda i: (i, 0)),
            pl.BlockSpec((tm, d), lambda i: (i, 0)),
            pl.BlockSpec((dff, d), lambda i: (0, 0), pipeline_mode=pl.Buffered(1)),
            pl.BlockSpec((None, 1, d), lambda i: (layer, 0, 0)),
            pl.BlockSpec((None, 1, d), lambda i: (layer, 0, 0)),
        ],
        out_specs=pl.BlockSpec((tm, d), lambda i: (i, 0)),
        out_shape=jax.ShapeDtypeStruct((m, d), F32),
        compiler_params=_cparams(("parallel",)),
        name="down",
    )(act, h, wd, lg, lb)


def _cast_rows(rows, steps):
    for n in range(min(steps, rows // 16), 0, -1):
        if rows % n == 0 and (rows // n) % 16 == 0:
            return rows // n
    return rows


def _largest_tile(n, cap, quantum):
    t = min(cap, n) // quantum * quantum
    while n % t:
        t -= quantum
    return t


def kernel(x, w_in, gla_w_alpha, gla_b_alpha, gla_norm_g, gla_w_o, conv_w, conv_b, lru_w_a, lru_b_a, lru_w_x,
           lru_b_x, lru_lambda, rnn_w_o, w_out, ln1_g, ln1_b, w_gate_up, w_down, ln2_g, ln2_b):
    batch, seq, d = x.shape
    depth = w_in.shape[0]
    m = batch * seq
    key_w = gla_w_alpha.shape[2]
    val_w = gla_w_o.shape[1]
    lru_w = conv_w.shape[2]
    dff = w_down.shape[1]
    alpha = (2.0 * depth) ** 0.25

    lr0 = 2 * key_w + 2 * val_w
    off = {"q": 0, "k": key_w, "v": 2 * key_w, "g": 2 * key_w + val_w,
           "rx": lr0, "ry": lr0 + lru_w, "ga": lr0 + 2 * lru_w, "gb": lr0 + 2 * lru_w + d}
    n_main = lr0 + 2 * lru_w + 2 * d

    tm_proj = _largest_tile(seq, 1024, 256)
    tn_proj = _largest_tile(n_main, 2048, 512)
    tc_gla = _largest_tile(seq, 2048, GLA_BLOCK)
    tc_lru = _largest_tile(seq, 1024, SUBLANES)
    tm_post = _largest_tile(seq, 256, 128)
    tm_gu = _largest_tile(m, 2048, 256)
    tf_gu = _largest_tile(dff, 512, 256)
    tm_down = _largest_tile(seq, 512, 128)
    ct_regroup = _largest_tile(lr0, 1024, LANES)
    w_in_t = jnp.swapaxes(w_in, 1, 2)

    row3 = lambda p: p[:, None, :]
    w_alpha = jnp.pad(gla_w_alpha, ((0, 0), (0, LANES - GLA_LOW_RANK), (0, 0))).astype(BF16)
    b_alpha, norm_g = row3(gla_b_alpha), row3(gla_norm_g)
    cb, ba, bx, lam = row3(conv_b), row3(lru_b_a), row3(lru_b_x), row3(lru_lambda)
    wa_b, wx_b = lru_w_a.astype(BF16), lru_w_x.astype(BF16)
    l1g, l1b, l2g, l2b = row3(ln1_g), row3(ln1_b), row3(ln2_g), row3(ln2_b)

    h = x.reshape(m, d)
    for l in range(depth):
        w_main, w_lr = _regroup(w_in_t, l, lr0, ct_regroup)
        proj, lr, woa_b, wob_b, wout_b = _in_proj(h, w_main, w_lr, (gla_w_o, rnn_w_o, w_out), l, tm_proj, tn_proj, off)
        za = _gla(proj, lr, w_alpha, b_alpha, norm_g, l, batch, seq, tc_gla, off)
        zb = _lru(proj, conv_w, cb, wa_b, ba, wx_b, bx, lam, l, batch, seq, tc_lru, 4 * LRU_BLOCK, off)
        h, hb = _post(za, zb, proj, h, woa_b, wob_b, wout_b, l1g, l1b, l, tm_post, off, alpha)
        act, wd_b = _gate_up(hb, w_gate_up, w_down, l, tm_gu, tf_gu)
        h = _down(act, h, wd_b, l2g, l2b, l, tm_down, alpha)
    return h.reshape(batch, seq, d)
```
